```python
import jax, jax.numpy as jnp
from jax import lax
import numpy as np

D_MODEL = 4096
BATCH = 1
SEQ = 16384
DEPTH = 1
DEC_BATCH = 1
DEC_SEQ = 8192
PAST_LEN = 128

GRID_W = 64
HEAD_DIM = 128
NA_HEADS = D_MODEL // 256
NA_WIDTH = NA_HEADS * HEAD_DIM
WIN_ROWS = 8
WIN_COLS = 16
LRU_WIDTH = D_MODEL // 2
LRU_BLOCKS = 16
LRU_BLOCK_W = LRU_WIDTH // LRU_BLOCKS
CONV_W = 4
LRU_C = 8.0
D_FF = -(-8 * D_MODEL // (3 * 256)) * 256
IN_WIDTH = 3 * NA_WIDTH + 2 * LRU_WIDTH
EPS = 1e-6

kernel_name = "hybrid_natten_rglru_encoder"


def rmsnorm(x, g):
    x32 = x.astype(jnp.float32)
    y = x32 * lax.rsqrt(jnp.mean(x32 * x32, axis=-1, keepdims=True) + EPS)
    return y.astype(x.dtype) * g


def neighbourhood_attention(q, k, v, rpb):
    b, L, h, dh = q.shape
    rows = L // GRID_W
    kr = min(WIN_ROWS, rows)
    qg = q.reshape(b, rows, GRID_W, h, dh)
    kg = k.reshape(b, rows, GRID_W, h, dh)
    vg = v.reshape(b, rows, GRID_W, h, dh)
    cols = jnp.arange(GRID_W)
    col_start = jnp.clip(cols - WIN_COLS // 2, 0, GRID_W - WIN_COLS)
    col_idx = col_start[:, None] + jnp.arange(WIN_COLS)[None, :]
    dc = col_idx - cols[:, None] + (WIN_COLS - 1)
    scale = dh ** -0.5

    def row_block(args):
        r, q_row = args
        rs = jnp.clip(r - kr // 2, 0, rows - kr)
        k_rows = lax.dynamic_slice_in_dim(kg, rs, kr, axis=1)
        v_rows = lax.dynamic_slice_in_dim(vg, rs, kr, axis=1)
        k_win = k_rows[:, :, col_idx]
        v_win = v_rows[:, :, col_idx]
        dr = rs + jnp.arange(kr) - r + (WIN_ROWS - 1)
        bias = rpb[:, dr[None, :, None], dc[:, None, :]]
        s = jnp.einsum('bqhd,brqwhd->bhqrw', q_row, k_win).astype(jnp.float32) * scale
        s = s + bias.astype(jnp.float32)[None]
        p = jax.nn.softmax(s.reshape(b, h, GRID_W, kr * WIN_COLS), axis=-1)
        p = p.reshape(s.shape).astype(v.dtype)
        return jnp.einsum('bhqrw,brqwhd->bqhd', p, v_win)

    out = lax.map(row_block, (jnp.arange(rows), jnp.moveaxis(qg, 1, 0)))
    return jnp.moveaxis(out, 0, 1).reshape(b, L, h * dh)


def depthwise_conv_centred(x, w, bias):
    L = x.shape[1]
    left = CONV_W // 2
    right = CONV_W - 1 - left
    xp = jnp.pad(x, ((0, 0), (left, right), (0, 0)))
    return sum(xp[:, j:j + L] * w[j] for j in range(CONV_W)) + bias


def rg_lru_direction(x, w_r, b_r, w_i, b_i, lam, reverse):
    b, L, c = x.shape
    xb = x.reshape(b, L, LRU_BLOCKS, LRU_BLOCK_W)
    r = jax.nn.sigmoid(jnp.einsum('blnc,ncm->blnm', xb, w_r).reshape(b, L, c) + b_r)
    i = jax.nn.sigmoid(jnp.einsum('blnc,ncm->blnm', xb, w_i).reshape(b, L, c) + b_i)
    log_a = (-LRU_C * r.astype(jnp.float32)) * jax.nn.softplus(-lam.astype(jnp.float32))
    a = jnp.exp(log_a)
    mult = jnp.sqrt(-jnp.expm1(2.0 * log_a))
    first = L - 1 if reverse else 0
    is_first = (jnp.arange(L) == first)[None, :, None]
    mult = jnp.where(is_first, 1.0, mult)
    u = mult * (i * x).astype(jnp.float32)

    def combine(e1, e2):
        a1, b1 = e1
        a2, b2 = e2
        return a1 * a2, a2 * b1 + b2

    _, h = lax.associative_scan(combine, (a, u), axis=1, reverse=reverse)
    return h


def swiglu(x, w_gate, w_up, w_down):
    return (jax.nn.silu(x @ w_gate) * (x @ w_up)) @ w_down


def encoder_layer(x, w_in, w_conv, b_conv, w_rgate, b_rgate, w_igate, b_igate, lru_lambda,
                  rpb, w_na_out, w_lru_out, w_merge, b_merge, w_out,
                  g_mix_pre, g_mix_post, g_ffn_pre, g_ffn_post, w_ffn_gate, w_ffn_up, w_ffn_down):
    b, L, _ = x.shape
    xn = rmsnorm(x, g_mix_pre)
    proj = xn @ w_in
    q, k, v, xr, gr = jnp.split(
        proj, [NA_WIDTH, 2 * NA_WIDTH, 3 * NA_WIDTH, 3 * NA_WIDTH + LRU_WIDTH], axis=-1)
    q = q.reshape(b, L, NA_HEADS, HEAD_DIM)
    k = k.reshape(b, L, NA_HEADS, HEAD_DIM)
    v = v.reshape(b, L, NA_HEADS, HEAD_DIM)
    na = neighbourhood_attention(q, k, v, rpb)

    xc = depthwise_conv_centred(xr, w_conv, b_conv)
    h = (rg_lru_direction(xc, w_rgate[0], b_rgate[0], w_igate[0], b_igate[0], lru_lambda[0], False)
         + rg_lru_direction(xc, w_rgate[1], b_rgate[1], w_igate[1], b_igate[1], lru_lambda[1], True))
    lru = h.astype(x.dtype) * jax.nn.gelu(gr)

    o_na = na @ w_na_out
    o_lru = lru @ w_lru_out
    gates = jax.nn.sigmoid(xn @ w_merge + b_merge)
    g_na, g_lru = jnp.split(gates, 2, axis=-1)
    mix = (g_na * o_na + g_lru * o_lru) @ w_out
    x = x + rmsnorm(mix, g_mix_post)
    f = swiglu(rmsnorm(x, g_ffn_pre), w_ffn_gate, w_ffn_up, w_ffn_down)
    return x + rmsnorm(f, g_ffn_post)


def setup_inputs(seed: int = 0) -> dict:
    key = jax.random.key(seed)
    ks = jax.random.split(key, 24)
    f32 = jnp.float32
    n = lambda k, shape, s: jax.random.normal(k, shape, f32) * s
    u = jax.random.uniform(ks[10], (DEPTH, 2, LRU_WIDTH), f32, minval=0.9, maxval=0.999)
    a0 = u ** (1.0 / LRU_C)
    lam = jnp.log(a0) - jnp.log1p(-a0)
    return {
        "x_prompt": n(ks[0], (BATCH, SEQ, D_MODEL), 1.0),
        "x_sample": n(ks[1], (DEC_BATCH, DEC_SEQ, D_MODEL), 1.0),
        "w_in": n(ks[2], (DEPTH, D_MODEL, IN_WIDTH), D_MODEL ** -0.5),
        "w_conv": n(ks[3], (DEPTH, CONV_W, LRU_WIDTH), CONV_W ** -0.5),
        "b_conv": n(ks[4], (DEPTH, LRU_WIDTH), 0.01),
        "w_rgate": n(ks[5], (DEPTH, 2, LRU_BLOCKS, LRU_BLOCK_W, LRU_BLOCK_W), LRU_BLOCK_W ** -0.5),
        "b_rgate": n(ks[6], (DEPTH, 2, LRU_WIDTH), 0.1),
        "w_igate": n(ks[7], (DEPTH, 2, LRU_BLOCKS, LRU_BLOCK_W, LRU_BLOCK_W), LRU_BLOCK_W ** -0.5),
        "b_igate": n(ks[8], (DEPTH, 2, LRU_WIDTH), 0.1),
        "lru_lambda": lam,
        "rpb": n(ks[9], (DEPTH, NA_HEADS, 2 * WIN_ROWS - 1, 2 * WIN_COLS - 1), 0.1),
        "w_na_out": n(ks[11], (DEPTH, NA_WIDTH, D_MODEL), NA_WIDTH ** -0.5),
        "w_lru_out": n(ks[12], (DEPTH, LRU_WIDTH, D_MODEL), LRU_WIDTH ** -0.5),
        "w_merge": n(ks[13], (DEPTH, D_MODEL, 2 * D_MODEL), D_MODEL ** -0.5),
        "b_merge": n(ks[14], (DEPTH, 2 * D_MODEL), 0.1),
        "w_out": n(ks[15], (DEPTH, D_MODEL, D_MODEL), D_MODEL ** -0.5),
        "g_mix_pre": 1.0 + n(ks[16], (DEPTH, D_MODEL), 0.02),
        "g_mix_post": 1.0 + n(ks[17], (DEPTH, D_MODEL), 0.02),
        "g_ffn_pre": 1.0 + n(ks[18], (DEPTH, D_MODEL), 0.02),
        "g_ffn_post": 1.0 + n(ks[19], (DEPTH, D_MODEL), 0.02),
        "w_ffn_gate": n(ks[20], (DEPTH, D_MODEL, D_FF), D_MODEL ** -0.5),
        "w_ffn_up": n(ks[21], (DEPTH, D_MODEL, D_FF), D_MODEL ** -0.5),
        "w_ffn_down": n(ks[22], (DEPTH, D_FF, D_MODEL), D_FF ** -0.5),
    }


def reference(x_prompt, x_sample, w_in, w_conv, b_conv, w_rgate, b_rgate, w_igate, b_igate,
              lru_lambda, rpb, w_na_out, w_lru_out, w_merge, b_merge, w_out,
              g_mix_pre, g_mix_post, g_ffn_pre, g_ffn_post, w_ffn_gate, w_ffn_up, w_ffn_down):
    y_prompt = x_prompt
    y_sample = x_sample
    for l in range(DEPTH):
        params = (w_in[l], w_conv[l], b_conv[l], w_rgate[l], b_rgate[l], w_igate[l], b_igate[l],
                  lru_lambda[l], rpb[l], w_na_out[l], w_lru_out[l], w_merge[l], b_merge[l], w_out[l],
                  g_mix_pre[l], g_mix_post[l], g_ffn_pre[l], g_ffn_post[l],
                  w_ffn_gate[l], w_ffn_up[l], w_ffn_down[l])
        y_prompt = encoder_layer(y_prompt, *params)
        y_sample = encoder_layer(y_sample, *params)
    return (y_prompt, y_sample)
```

```python
import functools

import jax
import jax.numpy as jnp
from jax import lax
from jax.experimental import pallas as pl
from jax.experimental.pallas import tpu as pltpu

F32 = jnp.float32
BF16 = jnp.bfloat16

GRID_W = 64
HEAD_DIM = 128
WIN_ROWS = 8
WIN_COLS = 16
CONV_W = 4
LRU_C = 8.0
EPS = 1e-6

LANES = 128
SUBLANES = 8
VMEM_LIMIT_BYTES = 56 * 1024 * 1024

NA_ROWS_PER_BLOCK = 4
NA_QBLK = NA_ROWS_PER_BLOCK * GRID_W
NA_KBLK = 3 * NA_QBLK
MASK_VALUE = -1e30


def _params(*semantics):
    return pltpu.CompilerParams(dimension_semantics=semantics, vmem_limit_bytes=VMEM_LIMIT_BYTES)


def _tile(n, pref):
    if n <= pref:
        return n
    t = (pref // LANES) * LANES
    while t >= LANES:
        if n % t == 0:
            return t
        t -= LANES
    return n


def _rms(x):
    return x * lax.rsqrt(jnp.mean(x * x, axis=-1, keepdims=True) + EPS)


def _rmsnorm_kernel(x_ref, g_ref, o_ref):
    o_ref[...] = (_rms(x_ref[...]) * g_ref[...]).astype(o_ref.dtype)


def _rmsnorm(x, g, out_dtype):
    L, D = x.shape
    bm = _tile(L, 256)
    return pl.pallas_call(
        _rmsnorm_kernel,
        grid=(L // bm,),
        in_specs=[pl.BlockSpec((bm, D), lambda i: (i, 0)), pl.BlockSpec((1, D), lambda i: (0, 0))],
        out_specs=pl.BlockSpec((bm, D), lambda i: (i, 0)),
        out_shape=jax.ShapeDtypeStruct((L, D), out_dtype),
        compiler_params=_params("parallel"),
        name="rmsnorm",
    )(x, g.reshape(1, D))


def _residual_norm_kernel(x_ref, y_ref, g_post_ref, g_next_ref, o_ref, on_ref):
    o = x_ref[...] + _rms(y_ref[...]) * g_post_ref[...]
    o_ref[...] = o
    on_ref[...] = (_rms(o) * g_next_ref[...]).astype(on_ref.dtype)


def _residual_kernel(x_ref, y_ref, g_post_ref, o_ref):
    o_ref[...] = x_ref[...] + _rms(y_ref[...]) * g_post_ref[...]


def _residual_norm(x, y, g_post, g_next=None):
    L, D = x.shape
    bm = _tile(L, 256)
    row = pl.BlockSpec((bm, D), lambda i: (i, 0))
    vec = pl.BlockSpec((1, D), lambda i: (0, 0))
    if g_next is None:
        return pl.pallas_call(
            _residual_kernel, grid=(L // bm,), in_specs=[row, row, vec], out_specs=row,
            out_shape=jax.ShapeDtypeStruct((L, D), F32), compiler_params=_params("parallel"),
            name="residual",
        )(x, y, g_post.reshape(1, D))
    return pl.pallas_call(
        _residual_norm_kernel, grid=(L // bm,), in_specs=[row, row, vec, vec], out_specs=[row, row],
        out_shape=[jax.ShapeDtypeStruct((L, D), F32), jax.ShapeDtypeStruct((L, D), BF16)],
        compiler_params=_params("parallel"), name="residual_norm",
    )(x, y, g_post.reshape(1, D), g_next.reshape(1, D))


def _mm_kernel(a_ref, b_ref, *rest, act, has_bias, head_major):
    if has_bias:
        bias_ref, o_ref = rest
    else:
        (o_ref,) = rest
    acc = jnp.dot(a_ref[...], b_ref[...], preferred_element_type=F32)
    if has_bias:
        acc = acc + bias_ref[...]
    if act == "sigmoid":
        acc = jax.nn.sigmoid(acc)
    if head_major:
        for c in range(o_ref.shape[0]):
            o_ref[c] = acc[:, c * LANES:(c + 1) * LANES].astype(o_ref.dtype)
    else:
        o_ref[...] = acc.astype(o_ref.dtype)


def _mm(a, b, *, out_dtype, bm, bn, bias=None, act=None, head_major=False):
    M, K = a.shape
    N = b.shape[1]
    bm = _tile(M, bm)
    bn = _tile(N, bn)
    in_specs = [pl.BlockSpec((bm, K), lambda i, j: (i, 0)), pl.BlockSpec((K, bn), lambda i, j: (0, j))]
    args = [a, b]
    if bias is not None:
        in_specs.append(pl.BlockSpec((1, bn), lambda i, j: (0, j)))
        args.append(bias.reshape(1, N).astype(F32))
    if head_major:
        out_specs = pl.BlockSpec((bn // LANES, bm, LANES), lambda i, j: (j, i, 0))
        out_shape = jax.ShapeDtypeStruct((N // LANES, M, LANES), out_dtype)
    else:
        out_specs = pl.BlockSpec((bm, bn), lambda i, j: (i, j))
        out_shape = jax.ShapeDtypeStruct((M, N), out_dtype)
    return pl.pallas_call(
        functools.partial(_mm_kernel, act=act, has_bias=bias is not None, head_major=head_major),
        grid=(M // bm, N // bn), in_specs=in_specs, out_specs=out_specs, out_shape=out_shape,
        compiler_params=_params("parallel", "arbitrary"), name="matmul",
    )(*args)


def _gated_out_kernel(na_ref, lru_ref, wna_ref, wlru_ref, gna_ref, glru_ref, o_ref):
    o_na = jnp.dot(na_ref[...], wna_ref[...], preferred_element_type=F32)
    o_lru = jnp.dot(lru_ref[...], wlru_ref[...], preferred_element_type=F32)
    o_ref[...] = (gna_ref[...].astype(F32) * o_na + glru_ref[...].astype(F32) * o_lru).astype(o_ref.dtype)


def _gated_out(na, lru, w_na, w_lru, gates, *, bm, bn):
    M, K = na.shape
    N = w_na.shape[1]
    bm = _tile(M, bm)
    bn = _tile(N, bn)
    nj = N // bn
    act = pl.BlockSpec((bm, K), lambda i, j: (i, 0))
    wgt = pl.BlockSpec((K, bn), lambda i, j: (0, j))
    return pl.pallas_call(
        _gated_out_kernel, grid=(M // bm, nj),
        in_specs=[act, act, wgt, wgt,
                  pl.BlockSpec((bm, bn), lambda i, j: (i, j)),
                  pl.BlockSpec((bm, bn), lambda i, j: (i, j + nj))],
        out_specs=pl.BlockSpec((bm, bn), lambda i, j: (i, j)),
        out_shape=jax.ShapeDtypeStruct((M, N), BF16),
        compiler_params=_params("parallel", "arbitrary"), name="gated_out",
    )(na, lru, w_na, w_lru, gates, gates)


def _swiglu_up_kernel(a_ref, wg_ref, wu_ref, o_ref):
    a = a_ref[...]
    g = jnp.dot(a, wg_ref[...], preferred_element_type=F32)
    u = jnp.dot(a, wu_ref[...], preferred_element_type=F32)
    o_ref[...] = (g * jax.nn.sigmoid(g) * u).astype(o_ref.dtype)


def _swiglu_up(a, w_gate, w_up, *, bm, bn):
    M, K = a.shape
    N = w_gate.shape[1]
    bm = _tile(M, bm)
    bn = _tile(N, bn)
    wgt = pl.BlockSpec((K, bn), lambda i, j: (0, j))
    return pl.pallas_call(
        _swiglu_up_kernel, grid=(M // bm, N // bn),
        in_specs=[pl.BlockSpec((bm, K), lambda i, j: (i, 0)), wgt, wgt],
        out_specs=pl.BlockSpec((bm, bn), lambda i, j: (i, j)),
        out_shape=jax.ShapeDtypeStruct((M, N), BF16),
        compiler_params=_params("parallel", "arbitrary"), name="swiglu_up",
    )(a, w_gate, w_up)


def _na_bias_tables(rpb, rows):
    heads = rpb.shape[0]
    nb = rows // NA_ROWS_PER_BLOCK

    def build(i):
        qr = i * NA_ROWS_PER_BLOCK + jnp.arange(NA_ROWS_PER_BLOCK)[:, None, None, None]
        qc = jnp.arange(GRID_W)[None, :, None, None]
        kr = (i - 1) * NA_ROWS_PER_BLOCK + jnp.arange(3 * NA_ROWS_PER_BLOCK)[None, None, :, None]
        kc = jnp.arange(GRID_W)[None, None, None, :]
        rs = jnp.clip(qr - WIN_ROWS // 2, 0, rows - WIN_ROWS)
        cs = jnp.clip(qc - WIN_COLS // 2, 0, GRID_W - WIN_COLS)
        valid = (kr >= rs) & (kr < rs + WIN_ROWS) & (kc >= cs) & (kc < cs + WIN_COLS)
        dr = jnp.clip(kr - qr + (WIN_ROWS - 1), 0, 2 * WIN_ROWS - 2)
        dc = jnp.clip(kc - qc + (WIN_COLS - 1), 0, 2 * WIN_COLS - 2)
        dr, dc, valid = jnp.broadcast_arrays(dr, dc, valid)
        b = jnp.where(valid[None], rpb[:, dr, dc].astype(F32), MASK_VALUE)
        return b.reshape(heads, NA_QBLK, NA_KBLK)

    return jnp.stack([build(0), build(1), build(nb - 1)])


def _na_kernel(q_ref, kp_ref, kc_ref, kn_ref, vp_ref, vc_ref, vn_ref, b_ref, o_ref, acc_ref, *, heads, scale):
    def head(h, carry):
        q = q_ref[h]
        s = []
        for seg, k_ref in enumerate((kp_ref, kc_ref, kn_ref)):
            qk = lax.dot_general(q, k_ref[h], (((1,), (1,)), ((), ())), preferred_element_type=F32)
            s.append(qk * scale + b_ref[0, h, :, seg * NA_QBLK:(seg + 1) * NA_QBLK])
        m = jnp.maximum(jnp.maximum(jnp.max(s[0], axis=-1, keepdims=True),
                                    jnp.max(s[1], axis=-1, keepdims=True)),
                        jnp.max(s[2], axis=-1, keepdims=True))
        p = [jnp.exp(x - m) for x in s]
        l = (jnp.sum(p[0], axis=-1, keepdims=True) + jnp.sum(p[1], axis=-1, keepdims=True)
             + jnp.sum(p[2], axis=-1, keepdims=True))
        o = jnp.dot(p[0].astype(BF16), vp_ref[h], preferred_element_type=F32)
        o = o + jnp.dot(p[1].astype(BF16), vc_ref[h], preferred_element_type=F32)
        o = o + jnp.dot(p[2].astype(BF16), vn_ref[h], preferred_element_type=F32)
        acc_ref[h] = (o / l).astype(acc_ref.dtype)
        return carry

    lax.fori_loop(0, heads, head, 0)
    for h in range(heads):
        o_ref[:, h * HEAD_DIM:(h + 1) * HEAD_DIM] = acc_ref[h]


def _neighbourhood_attention(qkv, bias, heads):
    L = qkv.shape[1]
    nb = L // NA_QBLK
    blk = (heads, NA_QBLK, HEAD_DIM)

    def prev_i(i):
        return jnp.maximum(i - 1, 0)

    def next_i(i):
        return jnp.minimum(i + 1, nb - 1)

    def btype(i):
        return jnp.where(i == 0, 0, jnp.where(i == nb - 1, 2, 1))

    in_specs = [
        pl.BlockSpec(blk, lambda i: (0, i, 0)),
        pl.BlockSpec(blk, lambda i: (1, prev_i(i), 0)),
        pl.BlockSpec(blk, lambda i: (1, i, 0)),
        pl.BlockSpec(blk, lambda i: (1, next_i(i), 0)),
        pl.BlockSpec(blk, lambda i: (2, prev_i(i), 0)),
        pl.BlockSpec(blk, lambda i: (2, i, 0)),
        pl.BlockSpec(blk, lambda i: (2, next_i(i), 0)),
        pl.BlockSpec((1, heads, NA_QBLK, NA_KBLK), lambda i: (btype(i), 0, 0, 0)),
    ]
    return pl.pallas_call(
        functools.partial(_na_kernel, heads=heads, scale=HEAD_DIM ** -0.5),
        grid=(nb,), in_specs=in_specs,
        out_specs=pl.BlockSpec((NA_QBLK, heads * HEAD_DIM), lambda i: (i, 0)),
        out_shape=jax.ShapeDtypeStruct((L, heads * HEAD_DIM), BF16),
        scratch_shapes=[pltpu.VMEM((heads, NA_QBLK, HEAD_DIM), BF16)],
        compiler_params=_params("arbitrary"), name="neighbourhood_attention",
    )(qkv, qkv, qkv, qkv, qkv, qkv, qkv, bias)


def _softplus(z):
    return jnp.maximum(z, 0.0) + jnp.log1p(jnp.exp(-jnp.abs(z)))


def _lru_kernel(x_ref, xp_ref, xn_ref, wc_ref, bc_ref, wg_ref, br_ref, bi_ref, lam_ref, *rest,
                reverse, seq_len):
    if reverse:
        hf_ref, gr_ref, o_ref, carry_ref, a_scr, h_scr = rest
    else:
        o_ref, carry_ref, a_scr, h_scr = rest
    tt, cb = x_ref.shape
    step = pl.program_id(1)
    nt = pl.num_programs(1)
    ti = nt - 1 - step if reverse else step

    x = x_ref[...]
    prev = jnp.where(ti == 0, 0.0, xp_ref[...])
    nxt = jnp.where(ti == nt - 1, 0.0, xn_ref[...])
    xcat = jnp.concatenate([prev, x, nxt], axis=0)
    w = wc_ref[...]
    h0 = SUBLANES
    xc = (xcat[h0 - 2:h0 - 2 + tt] * w[0:1] + xcat[h0 - 1:h0 - 1 + tt] * w[1:2]
          + x * w[2:3] + xcat[h0 + 1:h0 + 1 + tt] * w[3:4]) + bc_ref[...]

    xcb = xc.astype(BF16)
    gr_parts, gi_parts = [], []
    for n in range(cb // LANES):
        g = jnp.dot(xcb[:, n * LANES:(n + 1) * LANES], wg_ref[n], preferred_element_type=F32)
        gr_parts.append(g[:, :LANES])
        gi_parts.append(g[:, LANES:])
    r = jax.nn.sigmoid(jnp.concatenate(gr_parts, axis=1) + br_ref[...])
    ig = jax.nn.sigmoid(jnp.concatenate(gi_parts, axis=1) + bi_ref[...])

    log_a = (-LRU_C * r) * _softplus(-lam_ref[...])
    a = jnp.exp(log_a)
    th = jnp.tanh(log_a)
    mult = jnp.sqrt(-2.0 * th / (1.0 - th))
    t = ti * tt + lax.broadcasted_iota(jnp.int32, (tt, cb), 0)
    first = seq_len - 1 if reverse else 0
    mult = jnp.where(t == first, 1.0, mult)
    b = mult * (ig * xc)

    sub = lax.broadcasted_iota(jnp.int32, (tt, cb), 0) % SUBLANES
    for k in (1, 2, 4):
        shift = tt - k if reverse else k
        a_sh = pltpu.roll(a, shift, 0)
        b_sh = pltpu.roll(b, shift, 0)
        valid = (sub + k < SUBLANES) if reverse else (sub >= k)
        b = jnp.where(valid, a * b_sh + b, b)
        a = jnp.where(valid, a * a_sh, a)
    a_scr[...] = a
    h_scr[...] = b

    @pl.when(step == 0)
    def _():
        carry_ref[...] = jnp.zeros_like(carry_ref)

    groups = tt // SUBLANES
    edge = 0 if reverse else SUBLANES - 1

    def group(gi, carry):
        g = groups - 1 - gi if reverse else gi
        rows = pl.ds(pl.multiple_of(g * SUBLANES, SUBLANES), SUBLANES)
        h = a_scr[rows, :] * carry + h_scr[rows, :]
        h_scr[rows, :] = h
        return jnp.broadcast_to(h[edge:edge + 1, :], h.shape)

    carry_ref[...] = lax.fori_loop(0, groups, group, carry_ref[...])

    if reverse:
        h = hf_ref[...] + h_scr[...]
        o_ref[...] = (h * jax.nn.gelu(gr_ref[...])).astype(o_ref.dtype)
    else:
        o_ref[...] = h_scr[...]


def _rg_lru(xg, w_conv, b_conv, w_gates, b_r, b_i, lam, width):
    L = xg.shape[0]
    tt = _tile(L, 512)
    cb = _tile(width, 512)
    nt = L // tt
    nc = width // cb
    tpb = tt // SUBLANES
    nhalo = L // SUBLANES

    def run(reverse, extra_in, extra_specs, out_dtype):
        d = 1 if reverse else 0

        def tix(i):
            return nt - 1 - i if reverse else i

        vec = pl.BlockSpec((1, cb), lambda c, i: (0, c))
        in_specs = [
            pl.BlockSpec((tt, cb), lambda c, i: (tix(i), c)),
            pl.BlockSpec((SUBLANES, cb), lambda c, i: (jnp.maximum(tix(i) * tpb - 1, 0), c)),
            pl.BlockSpec((SUBLANES, cb), lambda c, i: (jnp.minimum((tix(i) + 1) * tpb, nhalo - 1), c)),
            pl.BlockSpec((CONV_W, cb), lambda c, i: (0, c)),
            vec,
            pl.BlockSpec((None, cb // LANES, LANES, 2 * LANES), lambda c, i: (d, c, 0, 0)),
            vec, vec, vec,
        ] + extra_specs(tix)
        return pl.pallas_call(
            functools.partial(_lru_kernel, reverse=reverse, seq_len=L),
            grid=(nc, nt), in_specs=in_specs,
            out_specs=pl.BlockSpec((tt, cb), lambda c, i: (tix(i), c)),
            out_shape=jax.ShapeDtypeStruct((L, width), out_dtype),
            scratch_shapes=[pltpu.VMEM((SUBLANES, cb), F32), pltpu.VMEM((tt, cb), F32),
                            pltpu.VMEM((tt, cb), F32)],
            compiler_params=_params("parallel", "arbitrary"),
            name="rg_lru_rev" if reverse else "rg_lru_fwd",
        )(xg, xg, xg, w_conv, b_conv.reshape(1, width), w_gates,
          b_r[d].reshape(1, width), b_i[d].reshape(1, width), lam[d].reshape(1, width), *extra_in)

    h_fwd = run(False, [], lambda tix: [], F32)
    return run(True, [h_fwd, xg],
               lambda tix: [pl.BlockSpec((tt, cb), lambda c, i: (tix(i), c)),
                            pl.BlockSpec((tt, cb), lambda c, i: (tix(i), c + nc))],
               BF16)


def _prepare_weights(w_in, w_rgate, w_igate, w_na_out, w_lru_out, w_merge, w_out,
                     w_ffn_gate, w_ffn_up, w_ffn_down):
    na_width = w_na_out.shape[0]
    d_ff = w_ffn_gate.shape[1]
    d_ff_pad = -(-d_ff // 512) * 512
    pad = d_ff_pad - d_ff
    w_gates = jnp.concatenate([w_rgate, w_igate], axis=-1).astype(BF16)
    return dict(
        w_qkv=w_in[:, :3 * na_width].astype(BF16),
        w_lru_in=w_in[:, 3 * na_width:].astype(BF16),
        w_merge=w_merge.astype(BF16),
        w_gates=w_gates,
        w_na_out=w_na_out.astype(BF16),
        w_lru_out=w_lru_out.astype(BF16),
        w_out=w_out.astype(BF16),
        w_ffn_gate=jnp.pad(w_ffn_gate, ((0, 0), (0, pad))).astype(BF16),
        w_ffn_up=jnp.pad(w_ffn_up, ((0, 0), (0, pad))).astype(BF16),
        w_ffn_down=jnp.pad(w_ffn_down, ((0, pad), (0, 0))).astype(BF16),
    )


def _encoder_layer(x, w, w_conv, b_conv, b_rgate, b_igate, lru_lambda, rpb, b_merge,
                   g_mix_pre, g_mix_post, g_ffn_pre, g_ffn_post):
    L, D = x.shape
    heads = w["w_na_out"].shape[0] // HEAD_DIM
    lru_width = w["w_lru_out"].shape[0]
    rows = L // GRID_W
    assert L % NA_QBLK == 0 and rows >= max(WIN_ROWS, 3 * NA_ROWS_PER_BLOCK)

    xn = _rmsnorm(x, g_mix_pre, BF16)
    qkv = _mm(xn, w["w_qkv"], out_dtype=BF16, bm=1024, bn=1024, head_major=True)
    xg = _mm(xn, w["w_lru_in"], out_dtype=F32, bm=1024, bn=1024)
    gates = _mm(xn, w["w_merge"], out_dtype=BF16, bm=1024, bn=1024, bias=b_merge, act="sigmoid")

    na = _neighbourhood_attention(qkv, _na_bias_tables(rpb, rows), heads)
    lru = _rg_lru(xg, w_conv, b_conv, w["w_gates"], b_rgate, b_igate, lru_lambda, lru_width)

    m = _gated_out(na, lru, w["w_na_out"], w["w_lru_out"], gates, bm=1024, bn=1024)
    mix = _mm(m, w["w_out"], out_dtype=F32, bm=1024, bn=1024)
    x1, x1n = _residual_norm(x, mix, g_mix_post, g_ffn_pre)

    h = _swiglu_up(x1n, w["w_ffn_gate"], w["w_ffn_up"], bm=1024, bn=512)
    f = _mm(h, w["w_ffn_down"], out_dtype=F32, bm=512, bn=512)
    return _residual_norm(x1, f, g_ffn_post)


def _per_batch(y, args):
    if y.shape[0] == 1:
        return _encoder_layer(y[0], *args)[None]
    return jnp.stack([_encoder_layer(y[b], *args) for b in range(y.shape[0])])


def kernel(x_prompt, x_sample, w_in, w_conv, b_conv, w_rgate, b_rgate, w_igate, b_igate, lru_lambda,
           rpb, w_na_out, w_lru_out, w_merge, b_merge, w_out, g_mix_pre, g_mix_post, g_ffn_pre,
           g_ffn_post, w_ffn_gate, w_ffn_up, w_ffn_down):
    y_prompt, y_sample = x_prompt, x_sample
    for l in range(w_in.shape[0]):
        w = _prepare_weights(w_in[l], w_rgate[l], w_igate[l], w_na_out[l], w_lru_out[l], w_merge[l],
                             w_out[l], w_ffn_gate[l], w_ffn_up[l], w_ffn_down[l])
        args = (w, w_conv[l], b_conv[l], b_rgate[l], b_igate[l], lru_lambda[l], rpb[l], b_merge[l],
                g_mix_pre[l], g_mix_post[l], g_ffn_pre[l], g_ffn_post[l])
        y_prompt = _per_batch(y_prompt, args)
        y_sample = _per_batch(y_sample, args)
    return (y_prompt, y_sample)
```

```python
import functools

import jax
import jax.numpy as jnp
import numpy as np
from jax import lax
from jax.experimental import pallas as pl
from jax.experimental.pallas import tpu as pltpu

F32 = jnp.float32
BF16 = jnp.bfloat16

GRID_W = 64
HEAD_DIM = 128
WIN_ROWS = 8
WIN_COLS = 16
CONV_W = 4
LRU_C = 8.0
EPS = 1e-6

LANES = 128
SUBLANES = 8
VMEM_LIMIT_BYTES = 56 * 1024 * 1024

NA_ROWS_PER_BLOCK = 4
NA_QBLK = NA_ROWS_PER_BLOCK * GRID_W
NA_KBLK = 3 * NA_QBLK
MASK_VALUE = -1e30
F32_TINY = float(np.finfo(np.float32).tiny)


def _params(*semantics):
    return pltpu.CompilerParams(dimension_semantics=semantics, vmem_limit_bytes=VMEM_LIMIT_BYTES)


def _tile(n, pref):
    if n <= pref:
        return n
    t = (pref // LANES) * LANES
    while t >= LANES:
        if n % t == 0:
            return t
        t -= LANES
    return n


def _rms(x):
    return x * lax.rsqrt(jnp.mean(x * x, axis=-1, keepdims=True) + EPS)


def _rmsnorm_kernel(x_ref, g_ref, o_ref):
    o_ref[...] = (_rms(x_ref[...]) * g_ref[...]).astype(o_ref.dtype)


def _rmsnorm(x, g, out_dtype):
    L, D = x.shape
    bm = _tile(L, 256)
    return pl.pallas_call(
        _rmsnorm_kernel,
        grid=(L // bm,),
        in_specs=[pl.BlockSpec((bm, D), lambda i: (i, 0)), pl.BlockSpec((1, D), lambda i: (0, 0))],
        out_specs=pl.BlockSpec((bm, D), lambda i: (i, 0)),
        out_shape=jax.ShapeDtypeStruct((L, D), out_dtype),
        compiler_params=_params("parallel"),
        name="rmsnorm",
    )(x, g.reshape(1, D))


def _residual_norm_kernel(x_ref, y_ref, g_post_ref, g_next_ref, o_ref, on_ref):
    o = x_ref[...] + _rms(y_ref[...]) * g_post_ref[...]
    o_ref[...] = o
    on_ref[...] = (_rms(o) * g_next_ref[...]).astype(on_ref.dtype)


def _residual_kernel(x_ref, y_ref, g_post_ref, o_ref):
    o_ref[...] = x_ref[...] + _rms(y_ref[...]) * g_post_ref[...]


def _residual_norm(x, y, g_post, g_next=None):
    L, D = x.shape
    bm = _tile(L, 256)
    row = pl.BlockSpec((bm, D), lambda i: (i, 0))
    vec = pl.BlockSpec((1, D), lambda i: (0, 0))
    if g_next is None:
        return pl.pallas_call(
            _residual_kernel, grid=(L // bm,), in_specs=[row, row, vec], out_specs=row,
            out_shape=jax.ShapeDtypeStruct((L, D), F32), compiler_params=_params("parallel"),
            name="residual",
        )(x, y, g_post.reshape(1, D))
    return pl.pallas_call(
        _residual_norm_kernel, grid=(L // bm,), in_specs=[row, row, vec, vec], out_specs=[row, row],
        out_shape=[jax.ShapeDtypeStruct((L, D), F32), jax.ShapeDtypeStruct((L, D), BF16)],
        compiler_params=_params("parallel"), name="residual_norm",
    )(x, y, g_post.reshape(1, D), g_next.reshape(1, D))


def _mm_kernel(a_ref, b_ref, *rest, act, has_bias, head_major, scaled_tiles, scale):
    if has_bias:
        bias_ref, o_ref = rest
    else:
        (o_ref,) = rest
    acc = jnp.dot(a_ref[...], b_ref[...], preferred_element_type=F32)
    if has_bias:
        acc = acc + bias_ref[...]
    if scaled_tiles:
        acc = acc * jnp.where(pl.program_id(1) < scaled_tiles, scale, 1.0)
    if act == "sigmoid":
        acc = jax.nn.sigmoid(acc)
    if head_major:
        for c in range(o_ref.shape[0]):
            o_ref[c] = acc[:, c * LANES:(c + 1) * LANES].astype(o_ref.dtype)
    else:
        o_ref[...] = acc.astype(o_ref.dtype)


def _mm(a, b, *, out_dtype, bm, bn, bias=None, act=None, head_major=False, scaled_cols=0, scale=1.0):
    M, K = a.shape
    N = b.shape[1]
    bm = _tile(M, bm)
    bn = _tile(N, bn)
    assert scaled_cols % bn == 0
    in_specs = [pl.BlockSpec((bm, K), lambda i, j: (i, 0)), pl.BlockSpec((K, bn), lambda i, j: (0, j))]
    args = [a, b]
    if bias is not None:
        in_specs.append(pl.BlockSpec((1, bn), lambda i, j: (0, j)))
        args.append(bias.reshape(1, N).astype(F32))
    if head_major:
        out_specs = pl.BlockSpec((bn // LANES, bm, LANES), lambda i, j: (j, i, 0))
        out_shape = jax.ShapeDtypeStruct((N // LANES, M, LANES), out_dtype)
    else:
        out_specs = pl.BlockSpec((bm, bn), lambda i, j: (i, j))
        out_shape = jax.ShapeDtypeStruct((M, N), out_dtype)
    return pl.pallas_call(
        functools.partial(_mm_kernel, act=act, has_bias=bias is not None, head_major=head_major,
                          scaled_tiles=scaled_cols // bn, scale=scale),
        grid=(M // bm, N // bn), in_specs=in_specs, out_specs=out_specs, out_shape=out_shape,
        compiler_params=_params("parallel", "arbitrary"), name="matmul",
    )(*args)


def _gated_out_kernel(na_ref, lru_ref, wna_ref, wlru_ref, gna_ref, glru_ref, o_ref):
    o_na = jnp.dot(na_ref[...], wna_ref[...], preferred_element_type=F32)
    o_lru = jnp.dot(lru_ref[...], wlru_ref[...], preferred_element_type=F32)
    o_ref[...] = (gna_ref[...].astype(F32) * o_na + glru_ref[...].astype(F32) * o_lru).astype(o_ref.dtype)


def _gated_out(na, lru, w_na, w_lru, gates, *, bm, bn):
    M, K = na.shape
    N = w_na.shape[1]
    bm = _tile(M, bm)
    bn = _tile(N, bn)
    nj = N // bn
    act = pl.BlockSpec((bm, K), lambda i, j: (i, 0))
    wgt = pl.BlockSpec((K, bn), lambda i, j: (0, j))
    return pl.pallas_call(
        _gated_out_kernel, grid=(M // bm, nj),
        in_specs=[act, act, wgt, wgt,
                  pl.BlockSpec((bm, bn), lambda i, j: (i, j)),
                  pl.BlockSpec((bm, bn), lambda i, j: (i, j + nj))],
        out_specs=pl.BlockSpec((bm, bn), lambda i, j: (i, j)),
        out_shape=jax.ShapeDtypeStruct((M, N), BF16),
        compiler_params=_params("parallel", "arbitrary"), name="gated_out",
    )(na, lru, w_na, w_lru, gates, gates)


def _swiglu_up_kernel(a_ref, wg_ref, wu_ref, o_ref):
    a = a_ref[...]
    g = jnp.dot(a, wg_ref[...], preferred_element_type=F32)
    u = jnp.dot(a, wu_ref[...], preferred_element_type=F32)
    o_ref[...] = (g * jax.nn.sigmoid(g) * u).astype(o_ref.dtype)


def _swiglu_up(a, w_gate, w_up, *, bm, bn):
    M, K = a.shape
    N = w_gate.shape[1]
    bm = _tile(M, bm)
    bn = _tile(N, bn)
    wgt = pl.BlockSpec((K, bn), lambda i, j: (0, j))
    return pl.pallas_call(
        _swiglu_up_kernel, grid=(M // bm, N // bn),
        in_specs=[pl.BlockSpec((bm, K), lambda i, j: (i, 0)), wgt, wgt],
        out_specs=pl.BlockSpec((bm, bn), lambda i, j: (i, j)),
        out_shape=jax.ShapeDtypeStruct((M, N), BF16),
        compiler_params=_params("parallel", "arbitrary"), name="swiglu_up",
    )(a, w_gate, w_up)


def _na_bias_tables(rpb, rows):
    heads = rpb.shape[0]
    rb = NA_ROWS_PER_BLOCK
    nb = rows // rb
    dc = np.arange(GRID_W)[None, :] - np.arange(GRID_W)[:, None] + (WIN_COLS - 1)
    t = sum(jnp.where(dc == d, rpb[:, :, d][:, :, None, None].astype(F32), 0.0)
            for d in range(2 * WIN_COLS - 1))
    off = WIN_ROWS - 1 - rb
    tb = jnp.stack([jnp.swapaxes(t[:, off - a:off - a + 3 * rb], 1, 2) for a in range(rb)], axis=1)

    def valid(i):
        qr = i * rb + np.arange(rb)[:, None, None, None]
        qc = np.arange(GRID_W)[None, :, None, None]
        kr = (i - 1) * rb + np.arange(3 * rb)[None, None, :, None]
        kc = np.arange(GRID_W)[None, None, None, :]
        rs = np.clip(qr - WIN_ROWS // 2, 0, rows - WIN_ROWS)
        cs = np.clip(qc - WIN_COLS // 2, 0, GRID_W - WIN_COLS)
        return (kr >= rs) & (kr < rs + WIN_ROWS) & (kc >= cs) & (kc < cs + WIN_COLS)

    tables = [jnp.where(valid(i)[None], tb, MASK_VALUE).reshape(heads, NA_QBLK, NA_KBLK)
              for i in (0, 1, nb - 1)]
    return jnp.stack(tables)


def _na_kernel(q_ref, kp_ref, kc_ref, kn_ref, vp_ref, vc_ref, vn_ref, b_ref, o_ref, acc_ref, *, heads):
    def head(h, carry):
        q = q_ref[h]
        s = []
        for seg, k_ref in enumerate((kp_ref, kc_ref, kn_ref)):
            qk = lax.dot_general(q, k_ref[h], (((1,), (1,)), ((), ())), preferred_element_type=F32)
            s.append(qk + b_ref[0, h, :, seg * NA_QBLK:(seg + 1) * NA_QBLK])
        m = jnp.max(jnp.maximum(jnp.maximum(s[0], s[1]), s[2]), axis=-1, keepdims=True)
        p = [jnp.exp(x - m) for x in s]
        l = jnp.sum(p[0] + p[1] + p[2], axis=-1, keepdims=True)
        o = jnp.dot(p[0].astype(BF16), vp_ref[h], preferred_element_type=F32)
        o = o + jnp.dot(p[1].astype(BF16), vc_ref[h], preferred_element_type=F32)
        o = o + jnp.dot(p[2].astype(BF16), vn_ref[h], preferred_element_type=F32)
        acc_ref[h] = (o / l).astype(acc_ref.dtype)
        return carry

    lax.fori_loop(0, heads, head, 0, unroll=2)
    for h in range(heads):
        o_ref[:, h * HEAD_DIM:(h + 1) * HEAD_DIM] = acc_ref[h]


def _neighbourhood_attention(qkv, bias, heads):
    L = qkv.shape[1]
    nb = L // NA_QBLK
    blk = (heads, NA_QBLK, HEAD_DIM)

    def prev_i(i):
        return jnp.maximum(i - 1, 0)

    def next_i(i):
        return jnp.minimum(i + 1, nb - 1)

    def btype(i):
        return jnp.where(i == 0, 0, jnp.where(i == nb - 1, 2, 1))

    in_specs = [
        pl.BlockSpec(blk, lambda i: (0, i, 0)),
        pl.BlockSpec(blk, lambda i: (1, prev_i(i), 0)),
        pl.BlockSpec(blk, lambda i: (1, i, 0)),
        pl.BlockSpec(blk, lambda i: (1, next_i(i), 0)),
        pl.BlockSpec(blk, lambda i: (2, prev_i(i), 0)),
        pl.BlockSpec(blk, lambda i: (2, i, 0)),
        pl.BlockSpec(blk, lambda i: (2, next_i(i), 0)),
        pl.BlockSpec((1, heads, NA_QBLK, NA_KBLK), lambda i: (btype(i), 0, 0, 0)),
    ]
    return pl.pallas_call(
        functools.partial(_na_kernel, heads=heads),
        grid=(nb,), in_specs=in_specs,
        out_specs=pl.BlockSpec((NA_QBLK, heads * HEAD_DIM), lambda i: (i, 0)),
        out_shape=jax.ShapeDtypeStruct((L, heads * HEAD_DIM), BF16),
        scratch_shapes=[pltpu.VMEM((heads, NA_QBLK, HEAD_DIM), BF16)],
        compiler_params=_params("arbitrary"), name="neighbourhood_attention",
    )(qkv, qkv, qkv, qkv, qkv, qkv, qkv, bias)


def _softplus(z):
    return jnp.maximum(z, 0.0) + jnp.log1p(jnp.exp(-jnp.abs(z)))


def _sigmoid(z):
    return 0.5 * jnp.tanh(0.5 * z) + 0.5


def _phase_scan(a_phases, b_phases, order):
    prods, hs = [None] * SUBLANES, [None] * SUBLANES
    p = h = None
    for n, j in enumerate(order):
        if n == 0:
            p, h = a_phases[j], b_phases[j]
        else:
            h = a_phases[j] * h + b_phases[j]
            p = a_phases[j] * p
        prods[j], hs[j] = p, h
    return prods, hs


def _group_carries(pt, ht, carry, scratch, reverse):
    m = pt.shape[0]
    order = list(range(SUBLANES))[::-1] if reverse else list(range(SUBLANES))
    if m == SUBLANES:
        sub = lax.broadcasted_iota(jnp.int32, pt.shape, 0)
        a, b = pt, ht
        for k in (1, 2, 4):
            shift = SUBLANES - k if reverse else k
            a_sh = pltpu.roll(a, shift, 0)
            b_sh = pltpu.roll(b, shift, 0)
            valid = (sub + k < SUBLANES) if reverse else (sub >= k)
            b = jnp.where(valid, a * b_sh + b, b)
            a = jnp.where(valid, a * a_sh, a)
        after = a * carry + b
        edge = SUBLANES - 1 if reverse else 0
        before = jnp.where(sub == edge, carry, pltpu.roll(after, SUBLANES - 1 if reverse else 1, 0))
        last = 0 if reverse else SUBLANES - 1
        return before, jnp.broadcast_to(after[last:last + 1, :], carry.shape)

    pt_ref, ht_ref, cin_ref = scratch[0]
    pt_ref[...] = pt
    ht_ref[...] = ht
    sub_m = m // SUBLANES
    a_ph = [pt_ref[pl.ds(j, sub_m, stride=SUBLANES), :] for j in range(SUBLANES)]
    b_ph = [ht_ref[pl.ds(j, sub_m, stride=SUBLANES), :] for j in range(SUBLANES)]
    prods, hs = _phase_scan(a_ph, b_ph, order)
    inner, carry_out = _group_carries(prods[order[-1]], hs[order[-1]], carry, scratch[1:], reverse)
    for n, j in enumerate(order):
        if n == 0:
            c = inner
        else:
            prev = order[n - 1]
            c = prods[prev] * inner + hs[prev]
        cin_ref[pl.ds(j, sub_m, stride=SUBLANES), :] = c
    return cin_ref[...], carry_out


def _lru_kernel(x_ref, xp_ref, xn_ref, wc_ref, bc_ref, wg_ref, br_ref, bi_ref, lam_ref, *rest, reverse):
    if reverse:
        hf_ref, gr_ref, o_ref, carry_ref, xs_ref, p_ref, h_ref, *levels = rest
    else:
        o_ref, carry_ref, xs_ref, p_ref, h_ref, *levels = rest
    scratch = [tuple(levels[i:i + 3]) for i in range(0, len(levels), 3)]
    tt = x_ref.shape[0]
    groups = tt // SUBLANES
    step = pl.program_id(1)
    nt = pl.num_programs(1)
    ti = nt - 1 - step if reverse else step
    order = list(range(SUBLANES))[::-1] if reverse else list(range(SUBLANES))

    xs_ref[0:SUBLANES, :] = jnp.where(ti == 0, 0.0, xp_ref[...])
    xs_ref[SUBLANES:SUBLANES + tt, :] = x_ref[...]
    xs_ref[SUBLANES + tt:, :] = jnp.where(ti == nt - 1, 0.0, xn_ref[...])

    xph = [xs_ref[pl.ds(SUBLANES + j, groups, stride=SUBLANES), :] for j in range(-2, SUBLANES + 1)]
    w = wc_ref[...]
    xc = jnp.concatenate(
        [xph[j] * w[0:1] + xph[j + 1] * w[1:2] + xph[j + 2] * w[2:3] + xph[j + 3] * w[3:4]
         for j in range(SUBLANES)], axis=0) + bc_ref[...]

    g = jnp.dot(xc.astype(BF16), wg_ref[...], preferred_element_type=F32)
    r = _sigmoid(g[:, :LANES] + br_ref[...])
    ig = _sigmoid(g[:, LANES:] + bi_ref[...])

    log_a = r * (-LRU_C * _softplus(-lam_ref[...]))
    a = jnp.exp(log_a)
    th = jnp.tanh(log_a)
    n = -2.0 * th
    mult = n * lax.rsqrt(jnp.maximum(n * (1.0 - th), F32_TINY))
    u = ig * xc
    b = mult * u

    sub = lax.broadcasted_iota(jnp.int32, (SUBLANES, LANES), 0)
    if reverse:
        lo = tt - SUBLANES
        fix = (sub == SUBLANES - 1) & (ti == nt - 1)
        b = jnp.concatenate([b[:lo], jnp.where(fix, u[lo:], b[lo:])], axis=0)
    else:
        fix = (sub == 0) & (ti == 0)
        b = jnp.concatenate([jnp.where(fix, u[:SUBLANES], b[:SUBLANES]), b[SUBLANES:]], axis=0)

    @pl.when(step == 0)
    def _():
        carry_ref[...] = jnp.zeros_like(carry_ref)

    a_ph = [a[j * groups:(j + 1) * groups] for j in range(SUBLANES)]
    b_ph = [b[j * groups:(j + 1) * groups] for j in range(SUBLANES)]
    prods, hs = _phase_scan(a_ph, b_ph, order)
    for j in range(SUBLANES):
        p_ref[j * groups:(j + 1) * groups, :] = prods[j]
        h_ref[j * groups:(j + 1) * groups, :] = hs[j]
    cin, carry_out = _group_carries(prods[order[-1]], hs[order[-1]], carry_ref[...], scratch, reverse)
    carry_ref[...] = carry_out

    dst = xs_ref if reverse else o_ref
    for j in range(SUBLANES):
        rows = slice(j * groups, (j + 1) * groups)
        dst[pl.ds(j, groups, stride=SUBLANES), :] = p_ref[rows, :] * cin + h_ref[rows, :]
    if reverse:
        h = hf_ref[...] + xs_ref[0:tt, :]
        o_ref[...] = (h * jax.nn.gelu(gr_ref[...])).astype(o_ref.dtype)


def _rg_lru(xg, w_conv, b_conv, w_gates, b_r, b_i, lam, width):
    L = xg.shape[0]
    tt = SUBLANES
    while tt * SUBLANES <= 4096 and L % (tt * SUBLANES) == 0:
        tt *= SUBLANES
    assert tt >= SUBLANES * SUBLANES and width % LANES == 0
    nt = L // tt
    nc = width // LANES
    tpb = tt // SUBLANES
    nhalo = L // SUBLANES

    scratch_shapes = [pltpu.VMEM((SUBLANES, LANES), F32), pltpu.VMEM((tt + 2 * SUBLANES, LANES), F32),
                      pltpu.VMEM((tt, LANES), F32), pltpu.VMEM((tt, LANES), F32)]
    m = tt // SUBLANES
    while m > SUBLANES:
        scratch_shapes += [pltpu.VMEM((m, LANES), F32)] * 3
        m //= SUBLANES

    def run(reverse, extra_in, extra_specs, out_dtype):
        d = 1 if reverse else 0

        def tix(i):
            return nt - 1 - i if reverse else i

        vec = pl.BlockSpec((1, LANES), lambda c, i: (0, c))
        in_specs = [
            pl.BlockSpec((tt, LANES), lambda c, i: (tix(i), c)),
            pl.BlockSpec((SUBLANES, LANES), lambda c, i: (jnp.maximum(tix(i) * tpb - 1, 0), c)),
            pl.BlockSpec((SUBLANES, LANES), lambda c, i: (jnp.minimum((tix(i) + 1) * tpb, nhalo - 1), c)),
            pl.BlockSpec((CONV_W, LANES), lambda c, i: (0, c)),
            vec,
            pl.BlockSpec((None, None, LANES, 2 * LANES), lambda c, i: (d, c, 0, 0)),
            vec, vec, vec,
        ] + extra_specs(tix)
        return pl.pallas_call(
            functools.partial(_lru_kernel, reverse=reverse),
            grid=(nc, nt), in_specs=in_specs,
            out_specs=pl.BlockSpec((tt, LANES), lambda c, i: (tix(i), c)),
            out_shape=jax.ShapeDtypeStruct((L, width), out_dtype),
            scratch_shapes=scratch_shapes,
            compiler_params=_params("parallel", "arbitrary"),
            name="rg_lru_rev" if reverse else "rg_lru_fwd",
        )(xg, xg, xg, w_conv, b_conv.reshape(1, width), w_gates,
          b_r[d].reshape(1, width), b_i[d].reshape(1, width), lam[d].reshape(1, width), *extra_in)

    h_fwd = run(False, [], lambda tix: [], F32)
    return run(True, [h_fwd, xg],
               lambda tix: [pl.BlockSpec((tt, LANES), lambda c, i: (tix(i), c)),
                            pl.BlockSpec((tt, LANES), lambda c, i: (tix(i), c + nc))],
               BF16)


def _prepare_weights(w_in, w_rgate, w_igate, w_na_out, w_lru_out, w_merge, w_out,
                     w_ffn_gate, w_ffn_up, w_ffn_down):
    na_width = w_na_out.shape[0]
    d_ff = w_ffn_gate.shape[1]
    d_ff_pad = -(-d_ff // 512) * 512
    pad = d_ff_pad - d_ff
    w_gates = jnp.concatenate([w_rgate, w_igate], axis=-1).astype(BF16)
    return dict(
        w_qkv=w_in[:, :3 * na_width].astype(BF16),
        w_lru_in=w_in[:, 3 * na_width:].astype(BF16),
        w_merge=w_merge.astype(BF16),
        w_gates=w_gates,
        w_na_out=w_na_out.astype(BF16),
        w_lru_out=w_lru_out.astype(BF16),
        w_out=w_out.astype(BF16),
        w_ffn_gate=jnp.pad(w_ffn_gate.astype(BF16), ((0, 0), (0, pad))),
        w_ffn_up=jnp.pad(w_ffn_up.astype(BF16), ((0, 0), (0, pad))),
        w_ffn_down=jnp.pad(w_ffn_down.astype(BF16), ((0, pad), (0, 0))),
    )


def _encoder_layer(x, w, w_conv, b_conv, b_rgate, b_igate, lru_lambda, rpb, b_merge,
                   g_mix_pre, g_mix_post, g_ffn_pre, g_ffn_post):
    L, D = x.shape
    heads = w["w_na_out"].shape[0] // HEAD_DIM
    lru_width = w["w_lru_out"].shape[0]
    rows = L // GRID_W
    assert L % NA_QBLK == 0 and rows >= max(WIN_ROWS, 3 * NA_ROWS_PER_BLOCK)

    xn = _rmsnorm(x, g_mix_pre, BF16)
    qkv = _mm(xn, w["w_qkv"], out_dtype=BF16, bm=1024, bn=_tile(heads * HEAD_DIM, 1024), head_major=True,
              scaled_cols=heads * HEAD_DIM, scale=HEAD_DIM ** -0.5)
    xg = _mm(xn, w["w_lru_in"], out_dtype=F32, bm=1024, bn=1024)
    gates = _mm(xn, w["w_merge"], out_dtype=BF16, bm=1024, bn=1024, bias=b_merge, act="sigmoid")

    na = _neighbourhood_attention(qkv, _na_bias_tables(rpb, rows), heads)
    lru = _rg_lru(xg, w_conv, b_conv, w["w_gates"], b_rgate, b_igate, lru_lambda, lru_width)

    m = _gated_out(na, lru, w["w_na_out"], w["w_lru_out"], gates, bm=1024, bn=1024)
    mix = _mm(m, w["w_out"], out_dtype=F32, bm=1024, bn=1024)
    x1, x1n = _residual_norm(x, mix, g_mix_post, g_ffn_pre)

    h = _swiglu_up(x1n, w["w_ffn_gate"], w["w_ffn_up"], bm=1024, bn=512)
    f = _mm(h, w["w_ffn_down"], out_dtype=F32, bm=512, bn=512)
    return _residual_norm(x1, f, g_ffn_post)


def _per_batch(y, args):
    if y.shape[0] == 1:
        return _encoder_layer(y[0], *args)[None]
    return jnp.stack([_encoder_layer(y[b], *args) for b in range(y.shape[0])])


def kernel(x_prompt, x_sample, w_in, w_conv, b_conv, w_rgate, b_rgate, w_igate, b_igate, lru_lambda,
           rpb, w_na_out, w_lru_out, w_merge, b_merge, w_out, g_mix_pre, g_mix_post, g_ffn_pre,
           g_ffn_post, w_ffn_gate, w_ffn_up, w_ffn_down):
    y_prompt, y_sample = x_prompt, x_sample
    for l in range(w_in.shape[0]):
        w = _prepare_weights(w_in[l], w_rgate[l], w_igate[l], w_na_out[l], w_lru_out[l], w_merge[l],
                             w_out[l], w_ffn_gate[l], w_ffn_up[l], w_ffn_down[l])
        args = (w, w_conv[l], b_conv[l], b_rgate[l], b_igate[l], lru_lambda[l], rpb[l], b_merge[l],
                g_mix_pre[l], g_mix_post[l], g_ffn_pre[l], g_ffn_post[l])
        y_prompt = _per_batch(y_prompt, args)
        y_sample = _per_batch(y_sample, args)
    return (y_prompt, y_sample)
```

```python
import functools

import jax
import jax.numpy as jnp
import numpy as np
from jax import lax
from jax.experimental import pallas as pl
from jax.experimental.pallas import tpu as pltpu

F32 = jnp.float32
BF16 = jnp.bfloat16

GRID_W = 64
HEAD_DIM = 128
WIN_ROWS = 8
WIN_COLS = 16
CONV_W = 4
LRU_C = 8.0
EPS = 1e-6

LANES = 128
SUBLANES = 8
VMEM_LIMIT_BYTES = 56 * 1024 * 1024

NA_ROWS_PER_BLOCK = 4
NA_QBLK = NA_ROWS_PER_BLOCK * GRID_W
NA_KBLK = 3 * NA_QBLK
MASK_VALUE = -1e30
F32_TINY = float(np.finfo(np.float32).tiny)


def _params(*semantics):
    return pltpu.CompilerParams(dimension_semantics=semantics, vmem_limit_bytes=VMEM_LIMIT_BYTES)


def _tile(n, pref):
    if n <= pref:
        return n
    t = (pref // LANES) * LANES
    while t >= LANES:
        if n % t == 0:
            return t
        t -= LANES
    return n


def _rms(x):
    return x * lax.rsqrt(jnp.mean(x * x, axis=-1, keepdims=True) + EPS)


def _sigmoid(z):
    return 0.5 * jnp.tanh(0.5 * z) + 0.5


def _rmsnorm_kernel(x_ref, g_ref, o_ref):
    o_ref[...] = (_rms(x_ref[...]) * g_ref[...]).astype(o_ref.dtype)


def _rmsnorm(x, g, out_dtype):
    L, D = x.shape
    bm = _tile(L, 256)
    return pl.pallas_call(
        _rmsnorm_kernel,
        grid=(L // bm,),
        in_specs=[pl.BlockSpec((bm, D), lambda i: (i, 0)), pl.BlockSpec((1, D), lambda i: (0, 0))],
        out_specs=pl.BlockSpec((bm, D), lambda i: (i, 0)),
        out_shape=jax.ShapeDtypeStruct((L, D), out_dtype),
        compiler_params=_params("parallel"),
        name="rmsnorm",
    )(x, g.reshape(1, D))


def _residual_kernel(x_ref, y_ref, g_post_ref, o_ref):
    o_ref[...] = x_ref[...] + _rms(y_ref[...].astype(F32)) * g_post_ref[...]


def _residual(x, y, g_post):
    L, D = x.shape
    bm = _tile(L, 256)
    row = pl.BlockSpec((bm, D), lambda i: (i, 0))
    return pl.pallas_call(
        _residual_kernel, grid=(L // bm,), in_specs=[row, row, pl.BlockSpec((1, D), lambda i: (0, 0))],
        out_specs=row, out_shape=jax.ShapeDtypeStruct((L, D), F32), compiler_params=_params("parallel"),
        name="residual",
    )(x, y, g_post.reshape(1, D))


def _out_residual_kernel(m_ref, w_ref, x_ref, g_post_ref, g_next_ref, o_ref, on_ref, *, parts):
    rows = m_ref.shape[0] // parts
    for p in range(parts):
        sl = slice(p * rows, (p + 1) * rows)
        mix = jnp.dot(m_ref[sl, :], w_ref[...], preferred_element_type=F32)
        o = x_ref[sl, :] + _rms(mix) * g_post_ref[...]
        o_ref[sl, :] = o
        on_ref[sl, :] = (_rms(o) * g_next_ref[...]).astype(on_ref.dtype)


def _out_residual(m, w, x, g_post, g_next, *, bm, parts):
    L, D = x.shape
    K = m.shape[1]
    bm = _tile(L, bm)
    row = lambda width: pl.BlockSpec((bm, width), lambda i: (i, 0))
    vec = pl.BlockSpec((1, D), lambda i: (0, 0))
    return pl.pallas_call(
        functools.partial(_out_residual_kernel, parts=parts), grid=(L // bm,),
        in_specs=[row(K), pl.BlockSpec((K, D), lambda i: (0, 0), pipeline_mode=pl.Buffered(1)),
                  row(D), vec, vec],
        out_specs=[row(D), row(D)],
        out_shape=[jax.ShapeDtypeStruct((L, D), F32), jax.ShapeDtypeStruct((L, D), BF16)],
        compiler_params=_params("parallel"), name="out_residual",
    )(m, w, x, g_post.reshape(1, D), g_next.reshape(1, D))


def _mm_kernel(a_ref, b_ref, *rest, act, has_bias, head_major, scaled_tiles, scale):
    if has_bias:
        bias_ref, o_ref = rest
    else:
        (o_ref,) = rest
    acc = jnp.dot(a_ref[...], b_ref[...], preferred_element_type=F32)
    if has_bias:
        acc = acc + bias_ref[...]
    if scaled_tiles:
        acc = acc * jnp.where(pl.program_id(1) < scaled_tiles, scale, 1.0)
    if act == "sigmoid":
        acc = _sigmoid(acc)
    if head_major:
        for c in range(o_ref.shape[0]):
            o_ref[c] = acc[:, c * LANES:(c + 1) * LANES].astype(o_ref.dtype)
    else:
        o_ref[...] = acc.astype(o_ref.dtype)


def _mm(a, b, *, out_dtype, bm, bn, bias=None, act=None, head_major=False, scaled_cols=0, scale=1.0):
    M, K = a.shape
    N = b.shape[1]
    bm = _tile(M, bm)
    bn = _tile(N, bn)
    assert scaled_cols % bn == 0
    in_specs = [pl.BlockSpec((bm, K), lambda i, j: (i, 0)), pl.BlockSpec((K, bn), lambda i, j: (0, j))]
    args = [a, b]
    if bias is not None:
        in_specs.append(pl.BlockSpec((1, bn), lambda i, j: (0, j)))
        args.append(bias.reshape(1, N).astype(F32))
    if head_major:
        out_specs = pl.BlockSpec((bn // LANES, bm, LANES), lambda i, j: (j, i, 0))
        out_shape = jax.ShapeDtypeStruct((N // LANES, M, LANES), out_dtype)
    else:
        out_specs = pl.BlockSpec((bm, bn), lambda i, j: (i, j))
        out_shape = jax.ShapeDtypeStruct((M, N), out_dtype)
    return pl.pallas_call(
        functools.partial(_mm_kernel, act=act, has_bias=bias is not None, head_major=head_major,
                          scaled_tiles=scaled_cols // bn, scale=scale),
        grid=(M // bm, N // bn), in_specs=in_specs, out_specs=out_specs, out_shape=out_shape,
        compiler_params=_params("parallel", "arbitrary"), name="matmul",
    )(*args)


def _gated_out_kernel(na_ref, lru_ref, wna_ref, wlru_ref, gna_ref, glru_ref, o_ref):
    o_na = jnp.dot(na_ref[...], wna_ref[...], preferred_element_type=F32)
    o_lru = jnp.dot(lru_ref[...], wlru_ref[...], preferred_element_type=F32)
    o_ref[...] = (gna_ref[...].astype(F32) * o_na + glru_ref[...].astype(F32) * o_lru).astype(o_ref.dtype)


def _gated_out(na, lru, w_na, w_lru, gates, *, bm, bn):
    M, K = na.shape
    N = w_na.shape[1]
    bm = _tile(M, bm)
    bn = _tile(N, bn)
    nj = N // bn
    act = pl.BlockSpec((bm, K), lambda i, j: (i, 0))
    wgt = pl.BlockSpec((K, bn), lambda i, j: (0, j))
    return pl.pallas_call(
        _gated_out_kernel, grid=(M // bm, nj),
        in_specs=[act, act, wgt, wgt,
                  pl.BlockSpec((bm, bn), lambda i, j: (i, j)),
                  pl.BlockSpec((bm, bn), lambda i, j: (i, j + nj))],
        out_specs=pl.BlockSpec((bm, bn), lambda i, j: (i, j)),
        out_shape=jax.ShapeDtypeStruct((M, N), BF16),
        compiler_params=_params("parallel", "arbitrary"), name="gated_out",
    )(na, lru, w_na, w_lru, gates, gates)


def _swiglu_up_kernel(a_ref, wg_ref, wu_ref, o_ref):
    a = a_ref[...]
    g = jnp.dot(a, wg_ref[...], preferred_element_type=F32)
    u = jnp.dot(a, wu_ref[...], preferred_element_type=F32)
    o_ref[...] = (g * _sigmoid(g) * u).astype(o_ref.dtype)


def _swiglu_up(a, w_gate, w_up, *, bm, bn):
    M, K = a.shape
    N = w_gate.shape[1]
    bm = _tile(M, bm)
    bn = min(bn, N)
    wgt = pl.BlockSpec((K, bn), lambda i, j: (0, j))
    return pl.pallas_call(
        _swiglu_up_kernel, grid=(M // bm, pl.cdiv(N, bn)),
        in_specs=[pl.BlockSpec((bm, K), lambda i, j: (i, 0)), wgt, wgt],
        out_specs=pl.BlockSpec((bm, bn), lambda i, j: (i, j)),
        out_shape=jax.ShapeDtypeStruct((M, N), BF16),
        compiler_params=_params("parallel", "arbitrary"), name="swiglu_up",
    )(a, w_gate, w_up)


def _na_bias_tables(rpb, rows):
    heads = rpb.shape[0]
    rb = NA_ROWS_PER_BLOCK
    nb = rows // rb
    dc = np.arange(GRID_W)[None, :] - np.arange(GRID_W)[:, None] + (WIN_COLS - 1)
    t = sum(jnp.where(dc == d, rpb[:, :, d][:, :, None, None].astype(F32), 0.0)
            for d in range(2 * WIN_COLS - 1))
    off = WIN_ROWS - 1 - rb
    tb = jnp.stack([jnp.swapaxes(t[:, off - a:off - a + 3 * rb], 1, 2) for a in range(rb)], axis=1)

    def valid(i):
        qr = i * rb + np.arange(rb)[:, None, None, None]
        qc = np.arange(GRID_W)[None, :, None, None]
        kr = (i - 1) * rb + np.arange(3 * rb)[None, None, :, None]
        kc = np.arange(GRID_W)[None, None, None, :]
        rs = np.clip(qr - WIN_ROWS // 2, 0, rows - WIN_ROWS)
        cs = np.clip(qc - WIN_COLS // 2, 0, GRID_W - WIN_COLS)
        return (kr >= rs) & (kr < rs + WIN_ROWS) & (kc >= cs) & (kc < cs + WIN_COLS)

    tables = [jnp.where(valid(i)[None], tb, MASK_VALUE).reshape(heads, NA_QBLK, NA_KBLK)
              for i in (0, 1, nb - 1)]
    return jnp.stack(tables)


def _na_kernel(q_ref, kp_ref, kc_ref, kn_ref, vp_ref, vc_ref, vn_ref, b_ref, o_ref, kcat_ref, vext_ref):
    @pl.when(pl.program_id(0) == 0)
    def _():
        vext_ref[:, :, HEAD_DIM:] = jnp.ones(vext_ref.shape[:2] + (HEAD_DIM,), BF16)

    for h in range(q_ref.shape[0]):
        for seg, (k_ref, v_ref) in enumerate(((kp_ref, vp_ref), (kc_ref, vc_ref), (kn_ref, vn_ref))):
            rows = slice(seg * NA_QBLK, (seg + 1) * NA_QBLK)
            kcat_ref[h, rows, :] = k_ref[h]
            vext_ref[h, rows, :HEAD_DIM] = v_ref[h]
        s = lax.dot_general(q_ref[h], kcat_ref[h], (((1,), (1,)), ((), ())),
                            preferred_element_type=F32) + b_ref[0, h]
        m = jnp.max(s, axis=-1, keepdims=True)
        p = jnp.exp(s - m).astype(BF16)
        oe = jnp.dot(p, vext_ref[h], preferred_element_type=F32)
        o_ref[:, h * HEAD_DIM:(h + 1) * HEAD_DIM] = (oe[:, :HEAD_DIM] / oe[:, HEAD_DIM:]).astype(o_ref.dtype)


def _neighbourhood_attention(qkv, bias, heads):
    L = qkv.shape[1]
    nb = L // NA_QBLK
    blk = (heads, NA_QBLK, HEAD_DIM)

    def prev_i(i):
        return jnp.maximum(i - 1, 0)

    def next_i(i):
        return jnp.minimum(i + 1, nb - 1)

    def btype(i):
        return jnp.where(i == 0, 0, jnp.where(i == nb - 1, 2, 1))

    in_specs = [
        pl.BlockSpec(blk, lambda i: (0, i, 0)),
        pl.BlockSpec(blk, lambda i: (1, prev_i(i), 0)),
        pl.BlockSpec(blk, lambda i: (1, i, 0)),
        pl.BlockSpec(blk, lambda i: (1, next_i(i), 0)),
        pl.BlockSpec(blk, lambda i: (2, prev_i(i), 0)),
        pl.BlockSpec(blk, lambda i: (2, i, 0)),
        pl.BlockSpec(blk, lambda i: (2, next_i(i), 0)),
        pl.BlockSpec((1, heads, NA_QBLK, NA_KBLK), lambda i: (btype(i), 0, 0, 0)),
    ]
    return pl.pallas_call(
        _na_kernel, grid=(nb,), in_specs=in_specs,
        out_specs=pl.BlockSpec((NA_QBLK, heads * HEAD_DIM), lambda i: (i, 0)),
        out_shape=jax.ShapeDtypeStruct((L, heads * HEAD_DIM), BF16),
        scratch_shapes=[pltpu.VMEM((heads, NA_KBLK, HEAD_DIM), BF16),
                        pltpu.VMEM((heads, NA_KBLK, 2 * HEAD_DIM), BF16)],
        compiler_params=_params("arbitrary"), name="neighbourhood_attention",
    )(qkv, qkv, qkv, qkv, qkv, qkv, qkv, bias)


def _softplus(z):
    return jnp.maximum(z, 0.0) + jnp.log1p(jnp.exp(-jnp.abs(z)))


def _phase_scan(a_phases, b_phases, order):
    prods, hs = [None] * SUBLANES, [None] * SUBLANES
    p = h = None
    for n, j in enumerate(order):
        if n == 0:
            p, h = a_phases[j], b_phases[j]
        else:
            h = a_phases[j] * h + b_phases[j]
            p = a_phases[j] * p
        prods[j], hs[j] = p, h
    return prods, hs


def _group_carries(pt, ht, carry, scratch, reverse):
    m = pt.shape[0]
    order = list(range(SUBLANES))[::-1] if reverse else list(range(SUBLANES))
    if m == SUBLANES:
        sub = lax.broadcasted_iota(jnp.int32, pt.shape, 0)
        a, b = pt, ht
        for k in (1, 2, 4):
            shift = SUBLANES - k if reverse else k
            a_sh = pltpu.roll(a, shift, 0)
            b_sh = pltpu.roll(b, shift, 0)
            valid = (sub + k < SUBLANES) if reverse else (sub >= k)
            b = jnp.where(valid, a * b_sh + b, b)
            a = jnp.where(valid, a * a_sh, a)
        after = a * carry + b
        edge = SUBLANES - 1 if reverse else 0
        before = jnp.where(sub == edge, carry, pltpu.roll(after, SUBLANES - 1 if reverse else 1, 0))
        last = 0 if reverse else SUBLANES - 1
        return before, jnp.broadcast_to(after[last:last + 1, :], carry.shape)

    pt_ref, ht_ref, cin_ref = scratch[0]
    pt_ref[...] = pt
    ht_ref[...] = ht
    sub_m = m // SUBLANES
    a_ph = [pt_ref[pl.ds(j, sub_m, stride=SUBLANES), :] for j in range(SUBLANES)]
    b_ph = [ht_ref[pl.ds(j, sub_m, stride=SUBLANES), :] for j in range(SUBLANES)]
    prods, hs = _phase_scan(a_ph, b_ph, order)
    inner, carry_out = _group_carries(prods[order[-1]], hs[order[-1]], carry, scratch[1:], reverse)
    for n, j in enumerate(order):
        if n == 0:
            c = inner
        else:
            prev = order[n - 1]
            c = prods[prev] * inner + hs[prev]
        cin_ref[pl.ds(j, sub_m, stride=SUBLANES), :] = c
    return cin_ref[...], carry_out


def _lru_kernel(x_ref, xp_ref, xn_ref, wc_ref, bc_ref, wg_ref, br_ref, bi_ref, lam_ref, *rest, reverse):
    if reverse:
        hf_ref, gr_ref, o_ref, carry_ref, xs_ref, p_ref, h_ref, *levels = rest
    else:
        o_ref, carry_ref, xs_ref, p_ref, h_ref, *levels = rest
    scratch = [tuple(levels[i:i + 3]) for i in range(0, len(levels), 3)]
    tt = x_ref.shape[0]
    groups = tt // SUBLANES
    step = pl.program_id(1)
    nt = pl.num_programs(1)
    ti = nt - 1 - step if reverse else step
    order = list(range(SUBLANES))[::-1] if reverse else list(range(SUBLANES))

    xs_ref[0:SUBLANES, :] = jnp.where(ti == 0, 0.0, xp_ref[...])
    xs_ref[SUBLANES:SUBLANES + tt, :] = x_ref[...]
    xs_ref[SUBLANES + tt:, :] = jnp.where(ti == nt - 1, 0.0, xn_ref[...])

    xph = [xs_ref[pl.ds(SUBLANES + j, groups, stride=SUBLANES), :] for j in range(-2, SUBLANES + 1)]
    w = wc_ref[...]
    xc = jnp.concatenate(
        [xph[j] * w[0:1] + xph[j + 1] * w[1:2] + xph[j + 2] * w[2:3] + xph[j + 3] * w[3:4]
         for j in range(SUBLANES)], axis=0) + bc_ref[...]

    g = jnp.dot(xc.astype(BF16), wg_ref[...], preferred_element_type=F32)
    r = _sigmoid(g[:, :LANES] + br_ref[...])
    ig = _sigmoid(g[:, LANES:] + bi_ref[...])

    log_a = r * (-LRU_C * _softplus(-lam_ref[...]))
    a = jnp.exp(log_a)
    th = jnp.tanh(log_a)
    n = -2.0 * th
    mult = n * lax.rsqrt(jnp.maximum(n * (1.0 - th), F32_TINY))
    u = ig * xc
    b = mult * u

    sub = lax.broadcasted_iota(jnp.int32, (SUBLANES, LANES), 0)
    if reverse:
        lo = tt - SUBLANES
        fix = (sub == SUBLANES - 1) & (ti == nt - 1)
        b = jnp.concatenate([b[:lo], jnp.where(fix, u[lo:], b[lo:])], axis=0)
    else:
        fix = (sub == 0) & (ti == 0)
        b = jnp.concatenate([jnp.where(fix, u[:SUBLANES], b[:SUBLANES]), b[SUBLANES:]], axis=0)

    @pl.when(step == 0)
    def _():
        carry_ref[...] = jnp.zeros_like(carry_ref)

    a_ph = [a[j * groups:(j + 1) * groups] for j in range(SUBLANES)]
    b_ph = [b[j * groups:(j + 1) * groups] for j in range(SUBLANES)]
    prods, hs = _phase_scan(a_ph, b_ph, order)
    for j in range(SUBLANES):
        p_ref[j * groups:(j + 1) * groups, :] = prods[j]
        h_ref[j * groups:(j + 1) * groups, :] = hs[j]
    cin, carry_out = _group_carries(prods[order[-1]], hs[order[-1]], carry_ref[...], scratch, reverse)
    carry_ref[...] = carry_out

    dst = xs_ref if reverse else o_ref
    for j in range(SUBLANES):
        rows = slice(j * groups, (j + 1) * groups)
        dst[pl.ds(j, groups, stride=SUBLANES), :] = p_ref[rows, :] * cin + h_ref[rows, :]
    if reverse:
        h = hf_ref[...] + xs_ref[0:tt, :]
        o_ref[...] = (h * jax.nn.gelu(gr_ref[...])).astype(o_ref.dtype)


def _rg_lru(xg, w_conv, b_conv, w_gates, b_r, b_i, lam, width):
    L = xg.shape[0]
    tt = SUBLANES
    while tt * SUBLANES <= 4096 and L % (tt * SUBLANES) == 0:
        tt *= SUBLANES
    assert tt >= SUBLANES * SUBLANES and width % LANES == 0
    nt = L // tt
    nc = width // LANES
    tpb = tt // SUBLANES
    nhalo = L // SUBLANES

    scratch_shapes = [pltpu.VMEM((SUBLANES, LANES), F32), pltpu.VMEM((tt + 2 * SUBLANES, LANES), F32),
                      pltpu.VMEM((tt, LANES), F32), pltpu.VMEM((tt, LANES), F32)]
    m = tt // SUBLANES
    while m > SUBLANES:
        scratch_shapes += [pltpu.VMEM((m, LANES), F32)] * 3
        m //= SUBLANES

    def run(reverse, extra_in, extra_specs, out_dtype):
        d = 1 if reverse else 0

        def tix(i):
            return nt - 1 - i if reverse else i

        vec = pl.BlockSpec((1, LANES), lambda c, i: (0, c))
        in_specs = [
            pl.BlockSpec((tt, LANES), lambda c, i: (tix(i), c)),
            pl.BlockSpec((SUBLANES, LANES), lambda c, i: (jnp.maximum(tix(i) * tpb - 1, 0), c)),
            pl.BlockSpec((SUBLANES, LANES), lambda c, i: (jnp.minimum((tix(i) + 1) * tpb, nhalo - 1), c)),
            pl.BlockSpec((CONV_W, LANES), lambda c, i: (0, c)),
            vec,
            pl.BlockSpec((None, None, LANES, 2 * LANES), lambda c, i: (d, c, 0, 0)),
            vec, vec, vec,
        ] + extra_specs(tix)
        return pl.pallas_call(
            functools.partial(_lru_kernel, reverse=reverse),
            grid=(nc, nt), in_specs=in_specs,
            out_specs=pl.BlockSpec((tt, LANES), lambda c, i: (tix(i), c)),
            out_shape=jax.ShapeDtypeStruct((L, width), out_dtype),
            scratch_shapes=scratch_shapes,
            compiler_params=_params("parallel", "arbitrary"),
            name="rg_lru_rev" if reverse else "rg_lru_fwd",
        )(xg, xg, xg, w_conv, b_conv.reshape(1, width), w_gates,
          b_r[d].reshape(1, width), b_i[d].reshape(1, width), lam[d].reshape(1, width), *extra_in)

    h_fwd = run(False, [], lambda tix: [], F32)
    return run(True, [h_fwd, xg],
               lambda tix: [pl.BlockSpec((tt, LANES), lambda c, i: (tix(i), c)),
                            pl.BlockSpec((tt, LANES), lambda c, i: (tix(i), c + nc))],
               BF16)


def _prepare_weights(w_in, w_rgate, w_igate, w_na_out, w_lru_out, w_merge, w_out,
                     w_ffn_gate, w_ffn_up, w_ffn_down):
    na_width = w_na_out.shape[0]
    w_gates = jnp.concatenate([w_rgate, w_igate], axis=-1).astype(BF16)
    return dict(
        w_qkv=w_in[:, :3 * na_width].astype(BF16),
        w_lru_in=w_in[:, 3 * na_width:].astype(BF16),
        w_merge=w_merge.astype(BF16),
        w_gates=w_gates,
        w_na_out=w_na_out.astype(BF16),
        w_lru_out=w_lru_out.astype(BF16),
        w_out=w_out.astype(BF16),
        w_ffn_gate=w_ffn_gate.astype(BF16),
        w_ffn_up=w_ffn_up.astype(BF16),
        w_ffn_down=w_ffn_down.astype(BF16),
    )


def _encoder_layer(x, w, w_conv, b_conv, b_rgate, b_igate, lru_lambda, rpb, b_merge,
                   g_mix_pre, g_mix_post, g_ffn_pre, g_ffn_post):
    L, D = x.shape
    heads = w["w_na_out"].shape[0] // HEAD_DIM
    lru_width = w["w_lru_out"].shape[0]
    rows = L // GRID_W
    assert L % NA_QBLK == 0 and rows >= max(WIN_ROWS, 3 * NA_ROWS_PER_BLOCK)

    xn = _rmsnorm(x, g_mix_pre, BF16)
    qkv = _mm(xn, w["w_qkv"], out_dtype=BF16, bm=1024, bn=_tile(heads * HEAD_DIM, 1024), head_major=True,
              scaled_cols=heads * HEAD_DIM, scale=HEAD_DIM ** -0.5)
    xg = _mm(xn, w["w_lru_in"], out_dtype=F32, bm=1024, bn=1024)
    gates = _mm(xn, w["w_merge"], out_dtype=BF16, bm=1024, bn=1024, bias=b_merge, act="sigmoid")

    na = _neighbourhood_attention(qkv, _na_bias_tables(rpb, rows), heads)
    lru = _rg_lru(xg, w_conv, b_conv, w["w_gates"], b_rgate, b_igate, lru_lambda, lru_width)

    m = _gated_out(na, lru, w["w_na_out"], w["w_lru_out"], gates, bm=1024, bn=1024)
    x1, x1n = _out_residual(m, w["w_out"], x, g_mix_post, g_ffn_pre, bm=128, parts=1)

    h = _swiglu_up(x1n, w["w_ffn_gate"], w["w_ffn_up"], bm=1024, bn=512)
    f = _mm(h, w["w_ffn_down"], out_dtype=BF16, bm=512, bn=512)
    return _residual(x1, f, g_ffn_post)


def _per_batch(y, args):
    if y.shape[0] == 1:
        return _encoder_layer(y[0], *args)[None]
    return jnp.stack([_encoder_layer(y[b], *args) for b in range(y.shape[0])])


def kernel(x_prompt, x_sample, w_in, w_conv, b_conv, w_rgate, b_rgate, w_igate, b_igate, lru_lambda,
           rpb, w_na_out, w_lru_out, w_merge, b_merge, w_out, g_mix_pre, g_mix_post, g_ffn_pre,
           g_ffn_post, w_ffn_gate, w_ffn_up, w_ffn_down):
    y_prompt, y_sample = x_prompt, x_sample
    for l in range(w_in.shape[0]):
        w = _prepare_weights(w_in[l], w_rgate[l], w_igate[l], w_na_out[l], w_lru_out[l], w_merge[l],
                             w_out[l], w_ffn_gate[l], w_ffn_up[l], w_ffn_down[l])
        args = (w, w_conv[l], b_conv[l], b_rgate[l], b_igate[l], lru_lambda[l], rpb[l], b_merge[l],
                g_mix_pre[l], g_mix_post[l], g_ffn_pre[l], g_ffn_post[l])
        y_prompt = _per_batch(y_prompt, args)
        y_sample = _per_batch(y_sample, args)
    return (y_prompt, y_sample)
```

```python
import functools
import math

import jax
import jax.numpy as jnp
import numpy as np
from jax import lax
from jax.experimental import pallas as pl
from jax.experimental.pallas import tpu as pltpu

F32 = jnp.float32
BF16 = jnp.bfloat16

GRID_W = 64
HEAD_DIM = 128
WIN_ROWS = 8
WIN_COLS = 16
CONV_W = 4
LRU_C = 8.0
EPS = 1e-6

LANES = 128
SUBLANES = 8
VMEM_LIMIT_BYTES = 56 * 1024 * 1024

NA_ROWS_PER_BLOCK = 4
NA_QBLK = NA_ROWS_PER_BLOCK * GRID_W
NA_KBLK = 3 * NA_QBLK
MASK_VALUE = -1e30
F32_TINY = float(np.finfo(np.float32).tiny)


def _params(*semantics):
    return pltpu.CompilerParams(dimension_semantics=semantics, vmem_limit_bytes=VMEM_LIMIT_BYTES)


def _tile(n, pref):
    if n <= pref:
        return n
    t = (pref // LANES) * LANES
    while t >= LANES:
        if n % t == 0:
            return t
        t -= LANES
    return n


def _rms(x):
    return x * lax.rsqrt(jnp.mean(x * x, axis=-1, keepdims=True) + EPS)


def _sigmoid(z):
    return 0.5 * jnp.tanh(0.5 * z) + 0.5


def _rmsnorm_kernel(x_ref, g_ref, o_ref):
    o_ref[...] = (_rms(x_ref[...]) * g_ref[...]).astype(o_ref.dtype)


def _rmsnorm(x, g, out_dtype):
    L, D = x.shape
    bm = _tile(L, 256)
    return pl.pallas_call(
        _rmsnorm_kernel,
        grid=(L // bm,),
        in_specs=[pl.BlockSpec((bm, D), lambda i: (i, 0)), pl.BlockSpec((1, D), lambda i: (0, 0))],
        out_specs=pl.BlockSpec((bm, D), lambda i: (i, 0)),
        out_shape=jax.ShapeDtypeStruct((L, D), out_dtype),
        compiler_params=_params("parallel"),
        name="rmsnorm",
    )(x, g.reshape(1, D))


def _residual_kernel(x_ref, y_ref, g_post_ref, o_ref):
    o_ref[...] = x_ref[...] + _rms(y_ref[...].astype(F32)) * g_post_ref[...]


def _residual(x, y, g_post):
    L, D = x.shape
    bm = _tile(L, 256)
    row = pl.BlockSpec((bm, D), lambda i: (i, 0))
    return pl.pallas_call(
        _residual_kernel, grid=(L // bm,), in_specs=[row, row, pl.BlockSpec((1, D), lambda i: (0, 0))],
        out_specs=row, out_shape=jax.ShapeDtypeStruct((L, D), F32), compiler_params=_params("parallel"),
        name="residual",
    )(x, y, g_post.reshape(1, D))


def _out_residual_kernel(m_ref, w_ref, x_ref, g_post_ref, g_next_ref, o_ref, on_ref, *, parts):
    rows = m_ref.shape[0] // parts
    for p in range(parts):
        sl = slice(p * rows, (p + 1) * rows)
        mix = jnp.dot(m_ref[sl, :], w_ref[...], preferred_element_type=F32)
        o = x_ref[sl, :] + _rms(mix) * g_post_ref[...]
        o_ref[sl, :] = o
        on_ref[sl, :] = (_rms(o) * g_next_ref[...]).astype(on_ref.dtype)


def _out_residual(m, w, x, g_post, g_next, *, bm, parts):
    L, D = x.shape
    K = m.shape[1]
    bm = _tile(L, bm)
    row = lambda width: pl.BlockSpec((bm, width), lambda i: (i, 0))
    vec = pl.BlockSpec((1, D), lambda i: (0, 0))
    return pl.pallas_call(
        functools.partial(_out_residual_kernel, parts=parts), grid=(L // bm,),
        in_specs=[row(K), pl.BlockSpec((K, D), lambda i: (0, 0), pipeline_mode=pl.Buffered(1)),
                  row(D), vec, vec],
        out_specs=[row(D), row(D)],
        out_shape=[jax.ShapeDtypeStruct((L, D), F32), jax.ShapeDtypeStruct((L, D), BF16)],
        compiler_params=_params("parallel"), name="out_residual",
    )(m, w, x, g_post.reshape(1, D), g_next.reshape(1, D))


def _mm_kernel(a_ref, b_ref, *rest, act, has_bias, head_major, scaled_tiles, scale):
    if has_bias:
        bias_ref, o_ref = rest
    else:
        (o_ref,) = rest
    acc = jnp.dot(a_ref[...], b_ref[...], preferred_element_type=F32)
    if has_bias:
        acc = acc + bias_ref[...]
    if scaled_tiles:
        acc = acc * jnp.where(pl.program_id(1) < scaled_tiles, scale, 1.0)
    if act == "sigmoid":
        acc = _sigmoid(acc)
    if head_major:
        for c in range(o_ref.shape[0]):
            o_ref[c] = acc[:, c * LANES:(c + 1) * LANES].astype(o_ref.dtype)
    else:
        o_ref[...] = acc.astype(o_ref.dtype)


def _mm(a, b, *, out_dtype, bm, bn, bias=None, act=None, head_major=False, scaled_cols=0, scale=1.0,
        cols=None):
    M, K = a.shape
    col0, N = cols if cols is not None else (0, b.shape[1])
    bm = _tile(M, bm)
    bn = _tile(N, bn)
    assert scaled_cols % bn == 0 and col0 % bn == 0
    j0 = col0 // bn
    in_specs = [pl.BlockSpec((bm, K), lambda i, j: (i, 0)), pl.BlockSpec((K, bn), lambda i, j: (0, j + j0))]
    args = [a, b]
    if bias is not None:
        in_specs.append(pl.BlockSpec((1, bn), lambda i, j: (0, j)))
        args.append(bias.reshape(1, N).astype(F32))
    if head_major:
        out_specs = pl.BlockSpec((bn // LANES, bm, LANES), lambda i, j: (j, i, 0))
        out_shape = jax.ShapeDtypeStruct((N // LANES, M, LANES), out_dtype)
    else:
        out_specs = pl.BlockSpec((bm, bn), lambda i, j: (i, j))
        out_shape = jax.ShapeDtypeStruct((M, N), out_dtype)
    return pl.pallas_call(
        functools.partial(_mm_kernel, act=act, has_bias=bias is not None, head_major=head_major,
                          scaled_tiles=scaled_cols // bn, scale=scale),
        grid=(M // bm, N // bn), in_specs=in_specs, out_specs=out_specs, out_shape=out_shape,
        compiler_params=_params("parallel", "arbitrary"), name="matmul",
    )(*args)


def _gated_out_kernel(na_ref, lru_ref, wna_ref, wlru_ref, gna_ref, glru_ref, o_ref):
    o_na = jnp.dot(na_ref[...], wna_ref[...], preferred_element_type=F32)
    o_lru = jnp.dot(lru_ref[...], wlru_ref[...], preferred_element_type=F32)
    o_ref[...] = (gna_ref[...].astype(F32) * o_na + glru_ref[...].astype(F32) * o_lru).astype(o_ref.dtype)


def _gated_out(na, lru, w_na, w_lru, gates, *, bm, bn):
    M, K = na.shape
    N = w_na.shape[1]
    bm = _tile(M, bm)
    bn = _tile(N, bn)
    nj = N // bn
    act = pl.BlockSpec((bm, K), lambda i, j: (i, 0))
    wgt = pl.BlockSpec((K, bn), lambda i, j: (0, j))
    return pl.pallas_call(
        _gated_out_kernel, grid=(M // bm, nj),
        in_specs=[act, act, wgt, wgt,
                  pl.BlockSpec((bm, bn), lambda i, j: (i, j)),
                  pl.BlockSpec((bm, bn), lambda i, j: (i, j + nj))],
        out_specs=pl.BlockSpec((bm, bn), lambda i, j: (i, j)),
        out_shape=jax.ShapeDtypeStruct((M, N), BF16),
        compiler_params=_params("parallel", "arbitrary"), name="gated_out",
    )(na, lru, w_na, w_lru, gates, gates)


def _swiglu_up_kernel(a_ref, wg_ref, wu_ref, o_ref):
    a = a_ref[...]
    g = jnp.dot(a, wg_ref[...], preferred_element_type=F32)
    u = jnp.dot(a, wu_ref[...], preferred_element_type=F32)
    o_ref[...] = (g * _sigmoid(g) * u).astype(o_ref.dtype)


def _swiglu_up(a, w_gate, w_up, *, bm, bn):
    M, K = a.shape
    N = w_gate.shape[1]
    bm = _tile(M, bm)
    bn = min(bn, N)
    wgt = pl.BlockSpec((K, bn), lambda i, j: (0, j))
    return pl.pallas_call(
        _swiglu_up_kernel, grid=(M // bm, pl.cdiv(N, bn)),
        in_specs=[pl.BlockSpec((bm, K), lambda i, j: (i, 0)), wgt, wgt],
        out_specs=pl.BlockSpec((bm, bn), lambda i, j: (i, j)),
        out_shape=jax.ShapeDtypeStruct((M, N), BF16),
        compiler_params=_params("parallel", "arbitrary"), name="swiglu_up",
    )(a, w_gate, w_up)


def _na_bias_tables(rpb, rows, cache):
    heads = rpb.shape[0]
    rb = NA_ROWS_PER_BLOCK
    nb = rows // rb

    def valid(i):
        qr = i * rb + np.arange(rb)[:, None, None, None]
        qc = np.arange(GRID_W)[None, :, None, None]
        kr = (i - 1) * rb + np.arange(3 * rb)[None, None, :, None]
        kc = np.arange(GRID_W)[None, None, None, :]
        rs = np.clip(qr - WIN_ROWS // 2, 0, rows - WIN_ROWS)
        cs = np.clip(qc - WIN_COLS // 2, 0, GRID_W - WIN_COLS)
        return (kr >= rs) & (kr < rs + WIN_ROWS) & (kc >= cs) & (kc < cs + WIN_COLS)

    masks = np.stack([valid(i) for i in (0, 1, nb - 1)])
    key = masks.tobytes()
    if key in cache:
        return cache[key]
    dc = np.arange(GRID_W)[None, :] - np.arange(GRID_W)[:, None] + (WIN_COLS - 1)
    t = sum(jnp.where(dc == d, rpb[:, :, d][:, :, None, None].astype(F32), 0.0)
            for d in range(2 * WIN_COLS - 1))
    off = WIN_ROWS - 1 - rb
    tb = jnp.stack([jnp.swapaxes(t[:, off - a:off - a + 3 * rb], 1, 2) for a in range(rb)], axis=1)
    tables = jnp.where(masks[:, None], tb[None], MASK_VALUE).reshape(3, heads, NA_QBLK, NA_KBLK)
    cache[key] = tables
    return tables


def _na_kernel(q_ref, kp_ref, kc_ref, kn_ref, vp_ref, vc_ref, vn_ref, b_ref, o_ref, kcat_ref, vext_ref):
    @pl.when(pl.program_id(0) == 0)
    def _():
        vext_ref[:, :, HEAD_DIM:] = jnp.ones(vext_ref.shape[:2] + (HEAD_DIM,), BF16)

    for h in range(q_ref.shape[0]):
        for seg, (k_ref, v_ref) in enumerate(((kp_ref, vp_ref), (kc_ref, vc_ref), (kn_ref, vn_ref))):
            rows = slice(seg * NA_QBLK, (seg + 1) * NA_QBLK)
            kcat_ref[h, rows, :] = k_ref[h]
            vext_ref[h, rows, :HEAD_DIM] = v_ref[h]
        s = lax.dot_general(q_ref[h], kcat_ref[h], (((1,), (1,)), ((), ())),
                            preferred_element_type=F32) + b_ref[0, h]
        m = jnp.max(s, axis=-1, keepdims=True)
        p = jnp.exp(s - m).astype(BF16)
        oe = jnp.dot(p, vext_ref[h], preferred_element_type=F32)
        o_ref[:, h * HEAD_DIM:(h + 1) * HEAD_DIM] = (oe[:, :HEAD_DIM] / oe[:, HEAD_DIM:]).astype(o_ref.dtype)


def _neighbourhood_attention(qkv, bias, heads):
    L = qkv.shape[1]
    nb = L // NA_QBLK
    blk = (heads, NA_QBLK, HEAD_DIM)

    def prev_i(i):
        return jnp.maximum(i - 1, 0)

    def next_i(i):
        return jnp.minimum(i + 1, nb - 1)

    def btype(i):
        return jnp.where(i == 0, 0, jnp.where(i == nb - 1, 2, 1))

    in_specs = [
        pl.BlockSpec(blk, lambda i: (0, i, 0)),
        pl.BlockSpec(blk, lambda i: (1, prev_i(i), 0)),
        pl.BlockSpec(blk, lambda i: (1, i, 0)),
        pl.BlockSpec(blk, lambda i: (1, next_i(i), 0)),
        pl.BlockSpec(blk, lambda i: (2, prev_i(i), 0)),
        pl.BlockSpec(blk, lambda i: (2, i, 0)),
        pl.BlockSpec(blk, lambda i: (2, next_i(i), 0)),
        pl.BlockSpec((1, heads, NA_QBLK, NA_KBLK), lambda i: (btype(i), 0, 0, 0)),
    ]
    return pl.pallas_call(
        _na_kernel, grid=(nb,), in_specs=in_specs,
        out_specs=pl.BlockSpec((NA_QBLK, heads * HEAD_DIM), lambda i: (i, 0)),
        out_shape=jax.ShapeDtypeStruct((L, heads * HEAD_DIM), BF16),
        scratch_shapes=[pltpu.VMEM((heads, NA_KBLK, HEAD_DIM), BF16),
                        pltpu.VMEM((heads, NA_KBLK, 2 * HEAD_DIM), BF16)],
        compiler_params=_params("arbitrary"), name="neighbourhood_attention",
    )(qkv, qkv, qkv, qkv, qkv, qkv, qkv, bias)


def _softplus(z):
    return jnp.maximum(z, 0.0) + jnp.log1p(jnp.exp(-jnp.abs(z)))


def _phase_scan(a_phases, b_phases, order):
    prods, hs = [None] * SUBLANES, [None] * SUBLANES
    p = h = None
    for n, j in enumerate(order):
        if n == 0:
            p, h = a_phases[j], b_phases[j]
        else:
            h = a_phases[j] * h + b_phases[j]
            p = a_phases[j] * p
        prods[j], hs[j] = p, h
    return prods, hs


def _group_carries(pt, ht, carry, scratch, reverse):
    m = pt.shape[0]
    order = list(range(SUBLANES))[::-1] if reverse else list(range(SUBLANES))
    if m == SUBLANES:
        sub = lax.broadcasted_iota(jnp.int32, pt.shape, 0)
        a, b = pt, ht
        for k in (1, 2, 4):
            shift = SUBLANES - k if reverse else k
            a_sh = pltpu.roll(a, shift, 0)
            b_sh = pltpu.roll(b, shift, 0)
            valid = (sub + k < SUBLANES) if reverse else (sub >= k)
            b = jnp.where(valid, a * b_sh + b, b)
            a = jnp.where(valid, a * a_sh, a)
        after = a * carry + b
        edge = SUBLANES - 1 if reverse else 0
        before = jnp.where(sub == edge, carry, pltpu.roll(after, SUBLANES - 1 if reverse else 1, 0))
        last = 0 if reverse else SUBLANES - 1
        return before, jnp.broadcast_to(after[last:last + 1, :], carry.shape)

    pt_ref, ht_ref, cin_ref = scratch[0]
    pt_ref[...] = pt
    ht_ref[...] = ht
    sub_m = m // SUBLANES
    a_ph = [pt_ref[pl.ds(j, sub_m, stride=SUBLANES), :] for j in range(SUBLANES)]
    b_ph = [ht_ref[pl.ds(j, sub_m, stride=SUBLANES), :] for j in range(SUBLANES)]
    prods, hs = _phase_scan(a_ph, b_ph, order)
    inner, carry_out = _group_carries(prods[order[-1]], hs[order[-1]], carry, scratch[1:], reverse)
    for n, j in enumerate(order):
        if n == 0:
            c = inner
        else:
            prev = order[n - 1]
            c = prods[prev] * inner + hs[prev]
        cin_ref[pl.ds(j, sub_m, stride=SUBLANES), :] = c
    return cin_ref[...], carry_out


def _lru_kernel(x_ref, xp_ref, xn_ref, wc_ref, bc_ref, wg_ref, br_ref, bi_ref, lam_ref, *rest, reverse):
    if reverse:
        hf_ref, gr_ref, o_ref, carry_ref, xs_ref, p_ref, h_ref, *levels = rest
    else:
        o_ref, carry_ref, xs_ref, p_ref, h_ref, *levels = rest
    scratch = [tuple(levels[i:i + 3]) for i in range(0, len(levels), 3)]
    tt = x_ref.shape[0]
    groups = tt // SUBLANES
    step = pl.program_id(1)
    nt = pl.num_programs(1)
    ti = nt - 1 - step if reverse else step
    order = list(range(SUBLANES))[::-1] if reverse else list(range(SUBLANES))

    xs_ref[0:SUBLANES, :] = jnp.where(ti == 0, 0.0, xp_ref[...])
    xs_ref[SUBLANES:SUBLANES + tt, :] = x_ref[...]
    xs_ref[SUBLANES + tt:, :] = jnp.where(ti == nt - 1, 0.0, xn_ref[...])

    xph = [xs_ref[pl.ds(SUBLANES + j, groups, stride=SUBLANES), :] for j in range(-2, SUBLANES + 1)]
    w = wc_ref[...]
    xc = jnp.concatenate(
        [xph[j] * w[0:1] + xph[j + 1] * w[1:2] + xph[j + 2] * w[2:3] + xph[j + 3] * w[3:4]
         for j in range(SUBLANES)], axis=0) + bc_ref[...]

    g = jnp.dot(xc.astype(BF16), wg_ref[...], preferred_element_type=F32)
    r = _sigmoid(g[:, :LANES] + br_ref[...])
    ig = _sigmoid(g[:, LANES:] + bi_ref[...])

    log_a = r * (-LRU_C * _softplus(-lam_ref[...]))
    a = jnp.exp(log_a)
    th = jnp.tanh(log_a)
    n = -2.0 * th
    mult = n * lax.rsqrt(jnp.maximum(n * (1.0 - th), F32_TINY))
    u = ig * xc
    b = mult * u

    sub = lax.broadcasted_iota(jnp.int32, (SUBLANES, LANES), 0)
    if reverse:
        lo = tt - SUBLANES
        fix = (sub == SUBLANES - 1) & (ti == nt - 1)
        b = jnp.concatenate([b[:lo], jnp.where(fix, u[lo:], b[lo:])], axis=0)
    else:
        fix = (sub == 0) & (ti == 0)
        b = jnp.concatenate([jnp.where(fix, u[:SUBLANES], b[:SUBLANES]), b[SUBLANES:]], axis=0)

    @pl.when(step == 0)
    def _():
        carry_ref[...] = jnp.zeros_like(carry_ref)

    a_ph = [a[j * groups:(j + 1) * groups] for j in range(SUBLANES)]
    b_ph = [b[j * groups:(j + 1) * groups] for j in range(SUBLANES)]
    prods, hs = _phase_scan(a_ph, b_ph, order)
    for j in range(SUBLANES):
        p_ref[j * groups:(j + 1) * groups, :] = prods[j]
        h_ref[j * groups:(j + 1) * groups, :] = hs[j]
    cin, carry_out = _group_carries(prods[order[-1]], hs[order[-1]], carry_ref[...], scratch, reverse)
    carry_ref[...] = carry_out

    dst = xs_ref if reverse else o_ref
    for j in range(SUBLANES):
        rows = slice(j * groups, (j + 1) * groups)
        dst[pl.ds(j, groups, stride=SUBLANES), :] = p_ref[rows, :] * cin + h_ref[rows, :]
    if reverse:
        h = hf_ref[...] + xs_ref[0:tt, :]
        o_ref[...] = (h * jax.nn.gelu(gr_ref[...])).astype(o_ref.dtype)


def _rg_lru(xg, w_conv, b_conv, w_gates, b_r, b_i, lam, width):
    L = xg.shape[0]
    tt = SUBLANES
    while tt * SUBLANES <= 4096 and L % (tt * SUBLANES) == 0:
        tt *= SUBLANES
    assert tt >= SUBLANES * SUBLANES and width % LANES == 0
    nt = L // tt
    nc = width // LANES
    tpb = tt // SUBLANES
    nhalo = L // SUBLANES

    scratch_shapes = [pltpu.VMEM((SUBLANES, LANES), F32), pltpu.VMEM((tt + 2 * SUBLANES, LANES), F32),
                      pltpu.VMEM((tt, LANES), F32), pltpu.VMEM((tt, LANES), F32)]
    m = tt // SUBLANES
    while m > SUBLANES:
        scratch_shapes += [pltpu.VMEM((m, LANES), F32)] * 3
        m //= SUBLANES

    def run(reverse, extra_in, extra_specs, out_dtype):
        d = 1 if reverse else 0

        def tix(i):
            return nt - 1 - i if reverse else i

        vec = pl.BlockSpec((1, LANES), lambda c, i: (0, c))
        in_specs = [
            pl.BlockSpec((tt, LANES), lambda c, i: (tix(i), c)),
            pl.BlockSpec((SUBLANES, LANES), lambda c, i: (jnp.maximum(tix(i) * tpb - 1, 0), c)),
            pl.BlockSpec((SUBLANES, LANES), lambda c, i: (jnp.minimum((tix(i) + 1) * tpb, nhalo - 1), c)),
            pl.BlockSpec((CONV_W, LANES), lambda c, i: (0, c)),
            vec,
            pl.BlockSpec((None, None, LANES, 2 * LANES), lambda c, i: (d, c, 0, 0)),
            vec, vec, vec,
        ] + extra_specs(tix)
        return pl.pallas_call(
            functools.partial(_lru_kernel, reverse=reverse),
            grid=(nc, nt), in_specs=in_specs,
            out_specs=pl.BlockSpec((tt, LANES), lambda c, i: (tix(i), c)),
            out_shape=jax.ShapeDtypeStruct((L, width), out_dtype),
            scratch_shapes=scratch_shapes,
            compiler_params=_params("parallel", "arbitrary"),
            name="rg_lru_rev" if reverse else "rg_lru_fwd",
        )(xg, xg, xg, w_conv, b_conv.reshape(1, width), w_gates,
          b_r[d].reshape(1, width), b_i[d].reshape(1, width), lam[d].reshape(1, width), *extra_in)

    h_fwd = run(False, [], lambda tix: [], F32)
    return run(True, [h_fwd, xg],
               lambda tix: [pl.BlockSpec((tt, LANES), lambda c, i: (tix(i), c)),
                            pl.BlockSpec((tt, LANES), lambda c, i: (tix(i), c + nc))],
               BF16)


def _prepare_weights(w_in, w_rgate, w_igate, w_na_out, w_lru_out, w_merge, w_out,
                     w_ffn_gate, w_ffn_up, w_ffn_down):
    w_gates = jnp.concatenate([w_rgate, w_igate], axis=-1).astype(BF16)
    return dict(
        w_in=w_in.astype(BF16),
        w_merge=w_merge.astype(BF16),
        w_gates=w_gates,
        w_na_out=w_na_out.astype(BF16),
        w_lru_out=w_lru_out.astype(BF16),
        w_out=w_out.astype(BF16),
        w_ffn_gate=w_ffn_gate.astype(BF16),
        w_ffn_up=w_ffn_up.astype(BF16),
        w_ffn_down=w_ffn_down.astype(BF16),
    )


def _encoder_layer(x, w, bias_cache, w_conv, b_conv, b_rgate, b_igate, lru_lambda, rpb, b_merge,
                   g_mix_pre, g_mix_post, g_ffn_pre, g_ffn_post):
    L, D = x.shape
    heads = w["w_na_out"].shape[0] // HEAD_DIM
    lru_width = w["w_lru_out"].shape[0]
    rows = L // GRID_W
    assert L % NA_QBLK == 0 and rows >= max(WIN_ROWS, 3 * NA_ROWS_PER_BLOCK)

    xn = _rmsnorm(x, g_mix_pre, BF16)
    na_width = heads * HEAD_DIM
    bn_in = _tile(math.gcd(na_width, lru_width), 1024)
    qkv = _mm(xn, w["w_in"], cols=(0, 3 * na_width), out_dtype=BF16, bm=1024, bn=bn_in, head_major=True,
              scaled_cols=na_width, scale=HEAD_DIM ** -0.5)
    xg = _mm(xn, w["w_in"], cols=(3 * na_width, 2 * lru_width), out_dtype=F32, bm=1024, bn=bn_in)
    gates = _mm(xn, w["w_merge"], out_dtype=BF16, bm=1024, bn=1024, bias=b_merge, act="sigmoid")

    na = _neighbourhood_attention(qkv, _na_bias_tables(rpb, rows, bias_cache), heads)
    lru = _rg_lru(xg, w_conv, b_conv, w["w_gates"], b_rgate, b_igate, lru_lambda, lru_width)

    m = _gated_out(na, lru, w["w_na_out"], w["w_lru_out"], gates, bm=1024, bn=1024)
    x1, x1n = _out_residual(m, w["w_out"], x, g_mix_post, g_ffn_pre, bm=128, parts=1)

    h = _swiglu_up(x1n, w["w_ffn_gate"], w["w_ffn_up"], bm=1024, bn=512)
    f = _mm(h, w["w_ffn_down"], out_dtype=BF16, bm=512, bn=512)
    return _residual(x1, f, g_ffn_post)


def _per_batch(y, args):
    if y.shape[0] == 1:
        return _encoder_layer(y[0], *args)[None]
    return jnp.stack([_encoder_layer(y[b], *args) for b in range(y.shape[0])])


def kernel(x_prompt, x_sample, w_in, w_conv, b_conv, w_rgate, b_rgate, w_igate, b_igate, lru_lambda,
           rpb, w_na_out, w_lru_out, w_merge, b_merge, w_out, g_mix_pre, g_mix_post, g_ffn_pre,
           g_ffn_post, w_ffn_gate, w_ffn_up, w_ffn_down):
    y_prompt, y_sample = x_prompt, x_sample
    for l in range(w_in.shape[0]):
        w = _prepare_weights(w_in[l], w_rgate[l], w_igate[l], w_na_out[l], w_lru_out[l], w_merge[l],
                             w_out[l], w_ffn_gate[l], w_ffn_up[l], w_ffn_down[l])
        args = (w, {}, w_conv[l], b_conv[l], b_rgate[l], b_igate[l], lru_lambda[l], rpb[l], b_merge[l],
                g_mix_pre[l], g_mix_post[l], g_ffn_pre[l], g_ffn_post[l])
        y_prompt = _per_batch(y_prompt, args)
        y_sample = _per_batch(y_sample, args)
    return (y_prompt, y_sample)
```

```python
import functools
import math

import jax
import jax.numpy as jnp
import numpy as np
from jax import lax
from jax.experimental import pallas as pl
from jax.experimental.pallas import tpu as pltpu

F32 = jnp.float32
BF16 = jnp.bfloat16

GRID_W = 64
HEAD_DIM = 128
WIN_ROWS = 8
WIN_COLS = 16
CONV_W = 4
LRU_C = 8.0
EPS = 1e-6

LANES = 128
SUBLANES = 8
VMEM_LIMIT_BYTES = 56 * 1024 * 1024

NA_ROWS_PER_BLOCK = 4
NA_QBLK = NA_ROWS_PER_BLOCK * GRID_W
NA_KBLK = 3 * NA_QBLK
MASK_VALUE = -1e30
F32_TINY = float(np.finfo(np.float32).tiny)


def _params(*semantics):
    return pltpu.CompilerParams(dimension_semantics=semantics, vmem_limit_bytes=VMEM_LIMIT_BYTES)


def _tile(n, pref):
    if n <= pref:
        return n
    t = (pref // LANES) * LANES
    while t >= LANES:
        if n % t == 0:
            return t
        t -= LANES
    return n


def _rms(x):
    return x * lax.rsqrt(jnp.mean(x * x, axis=-1, keepdims=True) + EPS)


def _sigmoid(z):
    return 0.5 * jnp.tanh(0.5 * z) + 0.5


def _cast_kernel(x_ref, o_ref):
    o_ref[...] = x_ref[...].astype(o_ref.dtype)


def _cast(x, dtype):
    R, C = x.shape
    br = _tile(R, 256)
    spec = pl.BlockSpec((br, C), lambda i: (i, 0))
    return pl.pallas_call(
        _cast_kernel, grid=(R // br,), in_specs=[spec], out_specs=spec,
        out_shape=jax.ShapeDtypeStruct((R, C), dtype), compiler_params=_params("parallel"), name="cast",
    )(x)


def _rmsnorm_kernel(x_ref, g_ref, o_ref):
    o_ref[...] = (_rms(x_ref[...]) * g_ref[...]).astype(o_ref.dtype)


def _rmsnorm(x, g, out_dtype):
    L, D = x.shape
    bm = _tile(L, 256)
    return pl.pallas_call(
        _rmsnorm_kernel,
        grid=(L // bm,),
        in_specs=[pl.BlockSpec((bm, D), lambda i: (i, 0)), pl.BlockSpec((1, D), lambda i: (0, 0))],
        out_specs=pl.BlockSpec((bm, D), lambda i: (i, 0)),
        out_shape=jax.ShapeDtypeStruct((L, D), out_dtype),
        compiler_params=_params("parallel"),
        name="rmsnorm",
    )(x, g.reshape(1, D))


def _residual_kernel(x_ref, y_ref, g_post_ref, o_ref):
    o_ref[...] = x_ref[...] + _rms(y_ref[...].astype(F32)) * g_post_ref[...]


def _residual(x, y, g_post):
    L, D = x.shape
    bm = _tile(L, 256)
    row = pl.BlockSpec((bm, D), lambda i: (i, 0))
    return pl.pallas_call(
        _residual_kernel, grid=(L // bm,), in_specs=[row, row, pl.BlockSpec((1, D), lambda i: (0, 0))],
        out_specs=row, out_shape=jax.ShapeDtypeStruct((L, D), F32), compiler_params=_params("parallel"),
        name="residual",
    )(x, y, g_post.reshape(1, D))


def _out_residual_kernel(m_ref, w_ref, x_ref, g_post_ref, g_next_ref, o_ref, on_ref, *, parts):
    rows = m_ref.shape[0] // parts
    for p in range(parts):
        sl = slice(p * rows, (p + 1) * rows)
        mix = jnp.dot(m_ref[sl, :], w_ref[...], preferred_element_type=F32)
        o = x_ref[sl, :] + _rms(mix) * g_post_ref[...]
        o_ref[sl, :] = o
        on_ref[sl, :] = (_rms(o) * g_next_ref[...]).astype(on_ref.dtype)


def _out_residual(m, w, x, g_post, g_next, *, bm, parts):
    L, D = x.shape
    K = m.shape[1]
    bm = _tile(L, bm)
    row = lambda width: pl.BlockSpec((bm, width), lambda i: (i, 0))
    vec = pl.BlockSpec((1, D), lambda i: (0, 0))
    return pl.pallas_call(
        functools.partial(_out_residual_kernel, parts=parts), grid=(L // bm,),
        in_specs=[row(K), pl.BlockSpec((K, D), lambda i: (0, 0), pipeline_mode=pl.Buffered(1)),
                  row(D), vec, vec],
        out_specs=[row(D), row(D)],
        out_shape=[jax.ShapeDtypeStruct((L, D), F32), jax.ShapeDtypeStruct((L, D), BF16)],
        compiler_params=_params("parallel"), name="out_residual",
    )(m, w, x, g_post.reshape(1, D), g_next.reshape(1, D))


def _mm_kernel(a_ref, b_ref, *rest, act, has_bias, head_major, scaled_tiles, scale):
    if has_bias:
        bias_ref, o_ref = rest
    else:
        (o_ref,) = rest
    acc = jnp.dot(a_ref[...], b_ref[...], preferred_element_type=F32)
    if has_bias:
        acc = acc + bias_ref[...]
    if scaled_tiles:
        acc = acc * jnp.where(pl.program_id(1) < scaled_tiles, scale, 1.0)
    if act == "sigmoid":
        acc = _sigmoid(acc)
    if head_major:
        for c in range(o_ref.shape[0]):
            o_ref[c] = acc[:, c * LANES:(c + 1) * LANES].astype(o_ref.dtype)
    else:
        o_ref[...] = acc.astype(o_ref.dtype)


def _mm(a, b, *, out_dtype, bm, bn, bias=None, act=None, head_major=False, scaled_cols=0, scale=1.0,
        cols=None):
    M, K = a.shape
    col0, N = cols if cols is not None else (0, b.shape[1])
    bm = _tile(M, bm)
    bn = _tile(N, bn)
    assert scaled_cols % bn == 0 and col0 % bn == 0
    j0 = col0 // bn
    in_specs = [pl.BlockSpec((bm, K), lambda i, j: (i, 0)), pl.BlockSpec((K, bn), lambda i, j: (0, j + j0))]
    args = [a, b]
    if bias is not None:
        in_specs.append(pl.BlockSpec((1, bn), lambda i, j: (0, j)))
        args.append(bias.reshape(1, N).astype(F32))
    if head_major:
        out_specs = pl.BlockSpec((bn // LANES, bm, LANES), lambda i, j: (j, i, 0))
        out_shape = jax.ShapeDtypeStruct((N // LANES, M, LANES), out_dtype)
    else:
        out_specs = pl.BlockSpec((bm, bn), lambda i, j: (i, j))
        out_shape = jax.ShapeDtypeStruct((M, N), out_dtype)
    return pl.pallas_call(
        functools.partial(_mm_kernel, act=act, has_bias=bias is not None, head_major=head_major,
                          scaled_tiles=scaled_cols // bn, scale=scale),
        grid=(M // bm, N // bn), in_specs=in_specs, out_specs=out_specs, out_shape=out_shape,
        compiler_params=_params("parallel", "arbitrary"), name="matmul",
    )(*args)


def _gated_out_kernel(na_ref, lru_ref, wna_ref, wlru_ref, gna_ref, glru_ref, o_ref):
    o_na = jnp.dot(na_ref[...], wna_ref[...], preferred_element_type=F32)
    o_lru = jnp.dot(lru_ref[...], wlru_ref[...], preferred_element_type=F32)
    o_ref[...] = (gna_ref[...].astype(F32) * o_na + glru_ref[...].astype(F32) * o_lru).astype(o_ref.dtype)


def _gated_out(na, lru, w_na, w_lru, gates, *, bm, bn):
    M, K = na.shape
    N = w_na.shape[1]
    bm = _tile(M, bm)
    bn = _tile(N, bn)
    nj = N // bn
    act = pl.BlockSpec((bm, K), lambda i, j: (i, 0))
    wgt = pl.BlockSpec((K, bn), lambda i, j: (0, j))
    return pl.pallas_call(
        _gated_out_kernel, grid=(M // bm, nj),
        in_specs=[act, act, wgt, wgt,
                  pl.BlockSpec((bm, bn), lambda i, j: (i, j)),
                  pl.BlockSpec((bm, bn), lambda i, j: (i, j + nj))],
        out_specs=pl.BlockSpec((bm, bn), lambda i, j: (i, j)),
        out_shape=jax.ShapeDtypeStruct((M, N), BF16),
        compiler_params=_params("parallel", "arbitrary"), name="gated_out",
    )(na, lru, w_na, w_lru, gates, gates)


def _swiglu_up_kernel(a_ref, wg_ref, wu_ref, o_ref):
    a = a_ref[...]
    g = jnp.dot(a, wg_ref[...], preferred_element_type=F32)
    u = jnp.dot(a, wu_ref[...], preferred_element_type=F32)
    o_ref[...] = (g * _sigmoid(g) * u).astype(o_ref.dtype)


def _swiglu_up(a, w_gate, w_up, *, bm, bn):
    M, K = a.shape
    N = w_gate.shape[1]
    bm = _tile(M, bm)
    bn = min(bn, N)
    wgt = pl.BlockSpec((K, bn), lambda i, j: (0, j))
    return pl.pallas_call(
        _swiglu_up_kernel, grid=(M // bm, pl.cdiv(N, bn)),
        in_specs=[pl.BlockSpec((bm, K), lambda i, j: (i, 0)), wgt, wgt],
        out_specs=pl.BlockSpec((bm, bn), lambda i, j: (i, j)),
        out_shape=jax.ShapeDtypeStruct((M, N), BF16),
        compiler_params=_params("parallel", "arbitrary"), name="swiglu_up",
    )(a, w_gate, w_up)


def _na_bias_tables(rpb, rows, cache):
    heads = rpb.shape[0]
    rb = NA_ROWS_PER_BLOCK
    nb = rows // rb

    def valid(i):
        qr = i * rb + np.arange(rb)[:, None, None, None]
        qc = np.arange(GRID_W)[None, :, None, None]
        kr = (i - 1) * rb + np.arange(3 * rb)[None, None, :, None]
        kc = np.arange(GRID_W)[None, None, None, :]
        rs = np.clip(qr - WIN_ROWS // 2, 0, rows - WIN_ROWS)
        cs = np.clip(qc - WIN_COLS // 2, 0, GRID_W - WIN_COLS)
        return (kr >= rs) & (kr < rs + WIN_ROWS) & (kc >= cs) & (kc < cs + WIN_COLS)

    masks = np.stack([valid(i) for i in (0, 1, nb - 1)])
    key = masks.tobytes()
    if key in cache:
        return cache[key]
    dc = np.arange(GRID_W)[None, :] - np.arange(GRID_W)[:, None] + (WIN_COLS - 1)
    t = sum(jnp.where(dc == d, rpb[:, :, d][:, :, None, None].astype(F32), 0.0)
            for d in range(2 * WIN_COLS - 1))
    off = WIN_ROWS - 1 - rb
    tb = jnp.stack([jnp.swapaxes(t[:, off - a:off - a + 3 * rb], 1, 2) for a in range(rb)], axis=1)
    tables = jnp.where(masks[:, None], tb[None], MASK_VALUE).reshape(3, heads, NA_QBLK, NA_KBLK)
    cache[key] = tables
    return tables


def _na_kernel(q_ref, kp_ref, kc_ref, kn_ref, vp_ref, vc_ref, vn_ref, b_ref, o_ref, kcat_ref, vext_ref):
    @pl.when(pl.program_id(0) == 0)
    def _():
        vext_ref[:, :, HEAD_DIM:] = jnp.ones(vext_ref.shape[:2] + (HEAD_DIM,), BF16)

    for h in range(q_ref.shape[0]):
        for seg, (k_ref, v_ref) in enumerate(((kp_ref, vp_ref), (kc_ref, vc_ref), (kn_ref, vn_ref))):
            rows = slice(seg * NA_QBLK, (seg + 1) * NA_QBLK)
            kcat_ref[h, rows, :] = k_ref[h]
            vext_ref[h, rows, :HEAD_DIM] = v_ref[h]
        s = lax.dot_general(q_ref[h], kcat_ref[h], (((1,), (1,)), ((), ())),
                            preferred_element_type=F32) + b_ref[0, h]
        m = jnp.max(s, axis=-1, keepdims=True)
        p = jnp.exp(s - m).astype(BF16)
        oe = jnp.dot(p, vext_ref[h], preferred_element_type=F32)
        o_ref[:, h * HEAD_DIM:(h + 1) * HEAD_DIM] = (oe[:, :HEAD_DIM] / oe[:, HEAD_DIM:]).astype(o_ref.dtype)


def _neighbourhood_attention(qkv, bias, heads):
    L = qkv.shape[1]
    nb = L // NA_QBLK
    blk = (heads, NA_QBLK, HEAD_DIM)

    def prev_i(i):
        return jnp.maximum(i - 1, 0)

    def next_i(i):
        return jnp.minimum(i + 1, nb - 1)

    def btype(i):
        return jnp.where(i == 0, 0, jnp.where(i == nb - 1, 2, 1))

    in_specs = [
        pl.BlockSpec(blk, lambda i: (0, i, 0)),
        pl.BlockSpec(blk, lambda i: (1, prev_i(i), 0)),
        pl.BlockSpec(blk, lambda i: (1, i, 0)),
        pl.BlockSpec(blk, lambda i: (1, next_i(i), 0)),
        pl.BlockSpec(blk, lambda i: (2, prev_i(i), 0)),
        pl.BlockSpec(blk, lambda i: (2, i, 0)),
        pl.BlockSpec(blk, lambda i: (2, next_i(i), 0)),
        pl.BlockSpec((1, heads, NA_QBLK, NA_KBLK), lambda i: (btype(i), 0, 0, 0)),
    ]
    return pl.pallas_call(
        _na_kernel, grid=(nb,), in_specs=in_specs,
        out_specs=pl.BlockSpec((NA_QBLK, heads * HEAD_DIM), lambda i: (i, 0)),
        out_shape=jax.ShapeDtypeStruct((L, heads * HEAD_DIM), BF16),
        scratch_shapes=[pltpu.VMEM((heads, NA_KBLK, HEAD_DIM), BF16),
                        pltpu.VMEM((heads, NA_KBLK, 2 * HEAD_DIM), BF16)],
        compiler_params=_params("arbitrary"), name="neighbourhood_attention",
    )(qkv, qkv, qkv, qkv, qkv, qkv, qkv, bias)


def _softplus(z):
    return jnp.maximum(z, 0.0) + jnp.log1p(jnp.exp(-jnp.abs(z)))


def _phase_scan(a_phases, b_phases, order):
    prods, hs = [None] * SUBLANES, [None] * SUBLANES
    p = h = None
    for n, j in enumerate(order):
        if n == 0:
            p, h = a_phases[j], b_phases[j]
        else:
            h = a_phases[j] * h + b_phases[j]
            p = a_phases[j] * p
        prods[j], hs[j] = p, h
    return prods, hs


def _group_carries(pt, ht, carry, scratch, reverse):
    m = pt.shape[0]
    order = list(range(SUBLANES))[::-1] if reverse else list(range(SUBLANES))
    if m == SUBLANES:
        sub = lax.broadcasted_iota(jnp.int32, pt.shape, 0)
        a, b = pt, ht
        for k in (1, 2, 4):
            shift = SUBLANES - k if reverse else k
            a_sh = pltpu.roll(a, shift, 0)
            b_sh = pltpu.roll(b, shift, 0)
            valid = (sub + k < SUBLANES) if reverse else (sub >= k)
            b = jnp.where(valid, a * b_sh + b, b)
            a = jnp.where(valid, a * a_sh, a)
        after = a * carry + b
        edge = SUBLANES - 1 if reverse else 0
        before = jnp.where(sub == edge, carry, pltpu.roll(after, SUBLANES - 1 if reverse else 1, 0))
        last = 0 if reverse else SUBLANES - 1
        return before, jnp.broadcast_to(after[last:last + 1, :], carry.shape)

    pt_ref, ht_ref, cin_ref = scratch[0]
    pt_ref[...] = pt
    ht_ref[...] = ht
    sub_m = m // SUBLANES
    a_ph = [pt_ref[pl.ds(j, sub_m, stride=SUBLANES), :] for j in range(SUBLANES)]
    b_ph = [ht_ref[pl.ds(j, sub_m, stride=SUBLANES), :] for j in range(SUBLANES)]
    prods, hs = _phase_scan(a_ph, b_ph, order)
    inner, carry_out = _group_carries(prods[order[-1]], hs[order[-1]], carry, scratch[1:], reverse)
    for n, j in enumerate(order):
        if n == 0:
            c = inner
        else:
            prev = order[n - 1]
            c = prods[prev] * inner + hs[prev]
        cin_ref[pl.ds(j, sub_m, stride=SUBLANES), :] = c
    return cin_ref[...], carry_out


def _lru_kernel(x_ref, xp_ref, xn_ref, wc_ref, bc_ref, wg_ref, br_ref, bi_ref, lam_ref, *rest, reverse):
    if reverse:
        hf_ref, gr_ref, o_ref, carry_ref, xs_ref, p_ref, h_ref, *levels = rest
    else:
        o_ref, carry_ref, xs_ref, p_ref, h_ref, *levels = rest
    scratch = [tuple(levels[i:i + 3]) for i in range(0, len(levels), 3)]
    tt = x_ref.shape[0]
    groups = tt // SUBLANES
    step = pl.program_id(1)
    nt = pl.num_programs(1)
    ti = nt - 1 - step if reverse else step
    order = list(range(SUBLANES))[::-1] if reverse else list(range(SUBLANES))

    xs_ref[0:SUBLANES, :] = jnp.where(ti == 0, 0.0, xp_ref[...])
    xs_ref[SUBLANES:SUBLANES + tt, :] = x_ref[...]
    xs_ref[SUBLANES + tt:, :] = jnp.where(ti == nt - 1, 0.0, xn_ref[...])

    xph = [xs_ref[pl.ds(SUBLANES + j, groups, stride=SUBLANES), :] for j in range(-2, SUBLANES + 1)]
    w = wc_ref[...]
    xc = jnp.concatenate(
        [xph[j] * w[0:1] + xph[j + 1] * w[1:2] + xph[j + 2] * w[2:3] + xph[j + 3] * w[3:4]
         for j in range(SUBLANES)], axis=0) + bc_ref[...]

    g = jnp.dot(xc.astype(BF16), wg_ref[...], preferred_element_type=F32)
    r = _sigmoid(g[:, :LANES] + br_ref[...])
    ig = _sigmoid(g[:, LANES:] + bi_ref[...])

    log_a = r * (-LRU_C * _softplus(-lam_ref[...]))
    a = jnp.exp(log_a)
    th = jnp.tanh(log_a)
    n = -2.0 * th
    mult = n * lax.rsqrt(jnp.maximum(n * (1.0 - th), F32_TINY))
    u = ig * xc
    b = mult * u

    sub = lax.broadcasted_iota(jnp.int32, (SUBLANES, LANES), 0)
    if reverse:
        lo = tt - SUBLANES
        fix = (sub == SUBLANES - 1) & (ti == nt - 1)
        b = jnp.concatenate([b[:lo], jnp.where(fix, u[lo:], b[lo:])], axis=0)
    else:
        fix = (sub == 0) & (ti == 0)
        b = jnp.concatenate([jnp.where(fix, u[:SUBLANES], b[:SUBLANES]), b[SUBLANES:]], axis=0)

    @pl.when(step == 0)
    def _():
        carry_ref[...] = jnp.zeros_like(carry_ref)

    a_ph = [a[j * groups:(j + 1) * groups] for j in range(SUBLANES)]
    b_ph = [b[j * groups:(j + 1) * groups] for j in range(SUBLANES)]
    prods, hs = _phase_scan(a_ph, b_ph, order)
    for j in range(SUBLANES):
        p_ref[j * groups:(j + 1) * groups, :] = prods[j]
        h_ref[j * groups:(j + 1) * groups, :] = hs[j]
    cin, carry_out = _group_carries(prods[order[-1]], hs[order[-1]], carry_ref[...], scratch, reverse)
    carry_ref[...] = carry_out

    dst = xs_ref if reverse else o_ref
    for j in range(SUBLANES):
        rows = slice(j * groups, (j + 1) * groups)
        dst[pl.ds(j, groups, stride=SUBLANES), :] = p_ref[rows, :] * cin + h_ref[rows, :]
    if reverse:
        h = hf_ref[...] + xs_ref[0:tt, :]
        o_ref[...] = (h * jax.nn.gelu(gr_ref[...])).astype(o_ref.dtype)


def _rg_lru(xg, w_conv, b_conv, w_gates, b_r, b_i, lam, width):
    L = xg.shape[0]
    tt = SUBLANES
    while tt * SUBLANES <= 4096 and L % (tt * SUBLANES) == 0:
        tt *= SUBLANES
    assert tt >= SUBLANES * SUBLANES and width % LANES == 0
    nt = L // tt
    nc = width // LANES
    tpb = tt // SUBLANES
    nhalo = L // SUBLANES

    scratch_shapes = [pltpu.VMEM((SUBLANES, LANES), F32), pltpu.VMEM((tt + 2 * SUBLANES, LANES), F32),
                      pltpu.VMEM((tt, LANES), F32), pltpu.VMEM((tt, LANES), F32)]
    m = tt // SUBLANES
    while m > SUBLANES:
        scratch_shapes += [pltpu.VMEM((m, LANES), F32)] * 3
        m //= SUBLANES

    def run(reverse, extra_in, extra_specs, out_dtype):
        d = 1 if reverse else 0

        def tix(i):
            return nt - 1 - i if reverse else i

        vec = pl.BlockSpec((1, LANES), lambda c, i: (0, c))
        in_specs = [
            pl.BlockSpec((tt, LANES), lambda c, i: (tix(i), c)),
            pl.BlockSpec((SUBLANES, LANES), lambda c, i: (jnp.maximum(tix(i) * tpb - 1, 0), c)),
            pl.BlockSpec((SUBLANES, LANES), lambda c, i: (jnp.minimum((tix(i) + 1) * tpb, nhalo - 1), c)),
            pl.BlockSpec((CONV_W, LANES), lambda c, i: (0, c)),
            vec,
            pl.BlockSpec((None, None, LANES, 2 * LANES), lambda c, i: (d, c, 0, 0)),
            vec, vec, vec,
        ] + extra_specs(tix)
        return pl.pallas_call(
            functools.partial(_lru_kernel, reverse=reverse),
            grid=(nc, nt), in_specs=in_specs,
            out_specs=pl.BlockSpec((tt, LANES), lambda c, i: (tix(i), c)),
            out_shape=jax.ShapeDtypeStruct((L, width), out_dtype),
            scratch_shapes=scratch_shapes,
            compiler_params=_params("parallel", "arbitrary"),
            name="rg_lru_rev" if reverse else "rg_lru_fwd",
        )(xg, xg, xg, w_conv, b_conv.reshape(1, width), w_gates,
          b_r[d].reshape(1, width), b_i[d].reshape(1, width), lam[d].reshape(1, width), *extra_in)

    h_fwd = run(False, [], lambda tix: [], F32)
    return run(True, [h_fwd, xg],
               lambda tix: [pl.BlockSpec((tt, LANES), lambda c, i: (tix(i), c)),
                            pl.BlockSpec((tt, LANES), lambda c, i: (tix(i), c + nc))],
               BF16)


def _prepare_weights(w_in, w_rgate, w_igate, w_na_out, w_lru_out, w_merge, w_out,
                     w_ffn_gate, w_ffn_up, w_ffn_down):
    w_gates = jnp.concatenate([w_rgate, w_igate], axis=-1).astype(BF16)
    return dict(
        w_in=_cast(w_in, BF16),
        w_merge=w_merge.astype(BF16),
        w_gates=w_gates,
        w_na_out=w_na_out.astype(BF16),
        w_lru_out=w_lru_out.astype(BF16),
        w_out=w_out.astype(BF16),
        w_ffn_gate=w_ffn_gate.astype(BF16),
        w_ffn_up=w_ffn_up.astype(BF16),
        w_ffn_down=w_ffn_down.astype(BF16),
    )


def _encoder_layer(x, w, bias_cache, w_conv, b_conv, b_rgate, b_igate, lru_lambda, rpb, b_merge,
                   g_mix_pre, g_mix_post, g_ffn_pre, g_ffn_post):
    L, D = x.shape
    heads = w["w_na_out"].shape[0] // HEAD_DIM
    lru_width = w["w_lru_out"].shape[0]
    rows = L // GRID_W
    assert L % NA_QBLK == 0 and rows >= max(WIN_ROWS, 3 * NA_ROWS_PER_BLOCK)

    xn = _rmsnorm(x, g_mix_pre, BF16)
    na_width = heads * HEAD_DIM
    bn_in = _tile(math.gcd(na_width, lru_width), 1024)
    qkv = _mm(xn, w["w_in"], cols=(0, 3 * na_width), out_dtype=BF16, bm=1024, bn=bn_in, head_major=True,
              scaled_cols=na_width, scale=HEAD_DIM ** -0.5)
    xg = _mm(xn, w["w_in"], cols=(3 * na_width, 2 * lru_width), out_dtype=F32, bm=1024, bn=bn_in)
    gates = _mm(xn, w["w_merge"], out_dtype=BF16, bm=1024, bn=1024, bias=b_merge, act="sigmoid")

    na = _neighbourhood_attention(qkv, _na_bias_tables(rpb, rows, bias_cache), heads)
    lru = _rg_lru(xg, w_conv, b_conv, w["w_gates"], b_rgate, b_igate, lru_lambda, lru_width)

    m = _gated_out(na, lru, w["w_na_out"], w["w_lru_out"], gates, bm=1024, bn=1024)
    x1, x1n = _out_residual(m, w["w_out"], x, g_mix_post, g_ffn_pre, bm=128, parts=1)

    h = _swiglu_up(x1n, w["w_ffn_gate"], w["w_ffn_up"], bm=1024, bn=512)
    f = _mm(h, w["w_ffn_down"], out_dtype=BF16, bm=512, bn=512)
    return _residual(x1, f, g_ffn_post)


def _per_batch(y, args):
    if y.shape[0] == 1:
        return _encoder_layer(y[0], *args)[None]
    return jnp.stack([_encoder_layer(y[b], *args) for b in range(y.shape[0])])


def kernel(x_prompt, x_sample, w_in, w_conv, b_conv, w_rgate, b_rgate, w_igate, b_igate, lru_lambda,
           rpb, w_na_out, w_lru_out, w_merge, b_merge, w_out, g_mix_pre, g_mix_post, g_ffn_pre,
           g_ffn_post, w_ffn_gate, w_ffn_up, w_ffn_down):
    y_prompt, y_sample = x_prompt, x_sample
    for l in range(w_in.shape[0]):
        w = _prepare_weights(w_in[l], w_rgate[l], w_igate[l], w_na_out[l], w_lru_out[l], w_merge[l],
                             w_out[l], w_ffn_gate[l], w_ffn_up[l], w_ffn_down[l])
        args = (w, {}, w_conv[l], b_conv[l], b_rgate[l], b_igate[l], lru_lambda[l], rpb[l], b_merge[l],
                g_mix_pre[l], g_mix_post[l], g_ffn_pre[l], g_ffn_post[l])
        y_prompt = _per_batch(y_prompt, args)
        y_sample = _per_batch(y_sample, args)
    return (y_prompt, y_sample)
```

```python
import functools
import math

import jax
import jax.numpy as jnp
import numpy as np
from jax import lax
from jax.experimental import pallas as pl
from jax.experimental.pallas import tpu as pltpu

F32 = jnp.float32
BF16 = jnp.bfloat16

GRID_W = 64
HEAD_DIM = 128
WIN_ROWS = 8
WIN_COLS = 16
CONV_W = 4
LRU_C = 8.0
EPS = 1e-6

LANES = 128
SUBLANES = 8
VMEM_BYTES = 64 * 1024 * 1024
VMEM_LIMIT_BYTES = 56 * 1024 * 1024

NA_ROWS_PER_BLOCK = 4
NA_QBLK = NA_ROWS_PER_BLOCK * GRID_W
NA_KBLK = 3 * NA_QBLK
MASK_VALUE = -1e30
F32_TINY = float(np.finfo(np.float32).tiny)


def _params(*semantics, vmem_limit_bytes=VMEM_LIMIT_BYTES):
    return pltpu.CompilerParams(dimension_semantics=semantics, vmem_limit_bytes=vmem_limit_bytes)


def _tile(n, pref):
    if n <= pref:
        return n
    t = (pref // LANES) * LANES
    while t >= LANES:
        if n % t == 0:
            return t
        t -= LANES
    return n


def _rms(x):
    return x * lax.rsqrt(jnp.mean(x * x, axis=-1, keepdims=True) + EPS)


def _sigmoid(z):
    return 0.5 * jnp.tanh(0.5 * z) + 0.5


def _rmsnorm_kernel(x_ref, g_ref, o_ref):
    o_ref[...] = (_rms(x_ref[...]) * g_ref[...]).astype(o_ref.dtype)


def _rmsnorm(x, g, out_dtype):
    L, D = x.shape
    bm = _tile(L, 256)
    return pl.pallas_call(
        _rmsnorm_kernel,
        grid=(L // bm,),
        in_specs=[pl.BlockSpec((bm, D), lambda i: (i, 0)), pl.BlockSpec((1, D), lambda i: (0, 0))],
        out_specs=pl.BlockSpec((bm, D), lambda i: (i, 0)),
        out_shape=jax.ShapeDtypeStruct((L, D), out_dtype),
        compiler_params=_params("parallel"),
        name="rmsnorm",
    )(x, g.reshape(1, D))


def _residual_kernel(x_ref, y_ref, g_post_ref, o_ref):
    o_ref[...] = x_ref[...] + _rms(y_ref[...].astype(F32)) * g_post_ref[...]


def _residual(x, y, g_post):
    L, D = x.shape
    bm = _tile(L, 256)
    row = pl.BlockSpec((bm, D), lambda i: (i, 0))
    return pl.pallas_call(
        _residual_kernel, grid=(L // bm,), in_specs=[row, row, pl.BlockSpec((1, D), lambda i: (0, 0))],
        out_specs=row, out_shape=jax.ShapeDtypeStruct((L, D), F32), compiler_params=_params("parallel"),
        name="residual",
    )(x, y, g_post.reshape(1, D))


def _out_residual_kernel(m_ref, w_ref, x_ref, g_post_ref, g_next_ref, o_ref, on_ref, *, parts):
    rows = m_ref.shape[0] // parts
    for p in range(parts):
        sl = slice(p * rows, (p + 1) * rows)
        mix = jnp.dot(m_ref[sl, :], w_ref[...], preferred_element_type=F32)
        o = x_ref[sl, :] + _rms(mix) * g_post_ref[...]
        o_ref[sl, :] = o
        on_ref[sl, :] = (_rms(o) * g_next_ref[...]).astype(on_ref.dtype)


def _out_residual(m, w, x, g_post, g_next, *, bm, parts):
    L, D = x.shape
    K = m.shape[1]
    bm = _tile(L, bm)
    row = lambda width: pl.BlockSpec((bm, width), lambda i: (i, 0))
    vec = pl.BlockSpec((1, D), lambda i: (0, 0))
    need = 2 * K * D + 2 * bm * (2 * K + 10 * D) + 4 * (bm // parts) * D
    limit = min(VMEM_BYTES - 4 * 1024 * 1024, max(VMEM_LIMIT_BYTES, need + 2 * 1024 * 1024))
    return pl.pallas_call(
        functools.partial(_out_residual_kernel, parts=parts), grid=(L // bm,),
        in_specs=[row(K), pl.BlockSpec((K, D), lambda i: (0, 0), pipeline_mode=pl.Buffered(1)),
                  row(D), vec, vec],
        out_specs=[row(D), row(D)],
        out_shape=[jax.ShapeDtypeStruct((L, D), F32), jax.ShapeDtypeStruct((L, D), BF16)],
        compiler_params=_params("parallel", vmem_limit_bytes=limit), name="out_residual",
    )(m, w, x, g_post.reshape(1, D), g_next.reshape(1, D))


def _mm_kernel(a_ref, b_ref, *rest, act, has_bias, head_major, scaled_tiles, scale):
    if has_bias:
        bias_ref, o_ref = rest
    else:
        (o_ref,) = rest
    acc = jnp.dot(a_ref[...], b_ref[...], preferred_element_type=F32)
    if has_bias:
        acc = acc + bias_ref[...]
    if scaled_tiles:
        acc = acc * jnp.where(pl.program_id(1) < scaled_tiles, scale, 1.0)
    if act == "sigmoid":
        acc = _sigmoid(acc)
    if head_major:
        for c in range(o_ref.shape[0]):
            o_ref[c] = acc[:, c * LANES:(c + 1) * LANES].astype(o_ref.dtype)
    else:
        o_ref[...] = acc.astype(o_ref.dtype)


def _mm(a, b, *, out_dtype, bm, bn, bias=None, act=None, head_major=False, scaled_cols=0, scale=1.0,
        cols=None):
    M, K = a.shape
    col0, N = cols if cols is not None else (0, b.shape[1])
    bm = _tile(M, bm)
    bn = _tile(N, bn)
    assert scaled_cols % bn == 0 and col0 % bn == 0
    j0 = col0 // bn
    in_specs = [pl.BlockSpec((bm, K), lambda i, j: (i, 0)), pl.BlockSpec((K, bn), lambda i, j: (0, j + j0))]
    args = [a, b]
    if bias is not None:
        in_specs.append(pl.BlockSpec((1, bn), lambda i, j: (0, j)))
        args.append(bias.reshape(1, N).astype(F32))
    if head_major:
        out_specs = pl.BlockSpec((bn // LANES, bm, LANES), lambda i, j: (j, i, 0))
        out_shape = jax.ShapeDtypeStruct((N // LANES, M, LANES), out_dtype)
    else:
        out_specs = pl.BlockSpec((bm, bn), lambda i, j: (i, j))
        out_shape = jax.ShapeDtypeStruct((M, N), out_dtype)
    return pl.pallas_call(
        functools.partial(_mm_kernel, act=act, has_bias=bias is not None, head_major=head_major,
                          scaled_tiles=scaled_cols // bn, scale=scale),
        grid=(M // bm, N // bn), in_specs=in_specs, out_specs=out_specs, out_shape=out_shape,
        compiler_params=_params("parallel", "arbitrary"), name="matmul",
    )(*args)


def _gated_out_kernel(na_ref, lru_ref, wna_ref, wlru_ref, gna_ref, glru_ref, o_ref):
    o_na = jnp.dot(na_ref[...], wna_ref[...], preferred_element_type=F32)
    o_lru = jnp.dot(lru_ref[...], wlru_ref[...], preferred_element_type=F32)
    o_ref[...] = (gna_ref[...].astype(F32) * o_na + glru_ref[...].astype(F32) * o_lru).astype(o_ref.dtype)


def _gated_out(na, lru, w_na, w_lru, gates, *, bm, bn):
    M, K = na.shape
    N = w_na.shape[1]
    bm = _tile(M, bm)
    bn = _tile(N, bn)
    nj = N // bn
    act = pl.BlockSpec((bm, K), lambda i, j: (i, 0))
    wgt = pl.BlockSpec((K, bn), lambda i, j: (0, j))
    return pl.pallas_call(
        _gated_out_kernel, grid=(M // bm, nj),
        in_specs=[act, act, wgt, wgt,
                  pl.BlockSpec((bm, bn), lambda i, j: (i, j)),
                  pl.BlockSpec((bm, bn), lambda i, j: (i, j + nj))],
        out_specs=pl.BlockSpec((bm, bn), lambda i, j: (i, j)),
        out_shape=jax.ShapeDtypeStruct((M, N), BF16),
        compiler_params=_params("parallel", "arbitrary"), name="gated_out",
    )(na, lru, w_na, w_lru, gates, gates)


def _swiglu_up_kernel(a_ref, wg_ref, wu_ref, o_ref):
    a = a_ref[...]
    g = jnp.dot(a, wg_ref[...], preferred_element_type=F32)
    u = jnp.dot(a, wu_ref[...], preferred_element_type=F32)
    o_ref[...] = (g * _sigmoid(g) * u).astype(o_ref.dtype)


def _swiglu_up(a, w_gate, w_up, *, bm, bn):
    M, K = a.shape
    N = w_gate.shape[1]
    bm = _tile(M, bm)
    bn = min(bn, N)
    wgt = pl.BlockSpec((K, bn), lambda i, j: (0, j))
    return pl.pallas_call(
        _swiglu_up_kernel, grid=(M // bm, pl.cdiv(N, bn)),
        in_specs=[pl.BlockSpec((bm, K), lambda i, j: (i, 0)), wgt, wgt],
        out_specs=pl.BlockSpec((bm, bn), lambda i, j: (i, j)),
        out_shape=jax.ShapeDtypeStruct((M, N), BF16),
        compiler_params=_params("parallel", "arbitrary"), name="swiglu_up",
    )(a, w_gate, w_up)


def _na_bias_tables(rpb, rows, cache):
    heads = rpb.shape[0]
    rb = NA_ROWS_PER_BLOCK
    nb = rows // rb

    def valid(i):
        qr = i * rb + np.arange(rb)[:, None, None, None]
        qc = np.arange(GRID_W)[None, :, None, None]
        kr = (i - 1) * rb + np.arange(3 * rb)[None, None, :, None]
        kc = np.arange(GRID_W)[None, None, None, :]
        rs = np.clip(qr - WIN_ROWS // 2, 0, rows - WIN_ROWS)
        cs = np.clip(qc - WIN_COLS // 2, 0, GRID_W - WIN_COLS)
        return (kr >= rs) & (kr < rs + WIN_ROWS) & (kc >= cs) & (kc < cs + WIN_COLS)

    masks = np.stack([valid(i) for i in (0, 1, nb - 1)])
    key = masks.tobytes()
    if key in cache:
        return cache[key]
    dc = np.arange(GRID_W)[None, :] - np.arange(GRID_W)[:, None] + (WIN_COLS - 1)
    t = sum(jnp.where(dc == d, rpb[:, :, d][:, :, None, None].astype(F32), 0.0)
            for d in range(2 * WIN_COLS - 1))
    off = WIN_ROWS - 1 - rb
    tb = jnp.stack([jnp.swapaxes(t[:, off - a:off - a + 3 * rb], 1, 2) for a in range(rb)], axis=1)
    tables = jnp.where(masks[:, None], tb[None], MASK_VALUE).reshape(3, heads, NA_QBLK, NA_KBLK)
    cache[key] = tables
    return tables


def _na_kernel(q_ref, kp_ref, kc_ref, kn_ref, vp_ref, vc_ref, vn_ref, b_ref, o_ref, kcat_ref, vext_ref):
    @pl.when(pl.program_id(0) == 0)
    def _():
        vext_ref[:, :, HEAD_DIM:] = jnp.ones(vext_ref.shape[:2] + (HEAD_DIM,), BF16)

    for h in range(q_ref.shape[0]):
        for seg, (k_ref, v_ref) in enumerate(((kp_ref, vp_ref), (kc_ref, vc_ref), (kn_ref, vn_ref))):
            rows = slice(seg * NA_QBLK, (seg + 1) * NA_QBLK)
            kcat_ref[h, rows, :] = k_ref[h]
            vext_ref[h, rows, :HEAD_DIM] = v_ref[h]
        s = lax.dot_general(q_ref[h], kcat_ref[h], (((1,), (1,)), ((), ())),
                            preferred_element_type=F32) + b_ref[0, h]
        m = jnp.max(s, axis=-1, keepdims=True)
        p = jnp.exp(s - m).astype(BF16)
        oe = jnp.dot(p, vext_ref[h], preferred_element_type=F32)
        o_ref[:, h * HEAD_DIM:(h + 1) * HEAD_DIM] = (oe[:, :HEAD_DIM] / oe[:, HEAD_DIM:]).astype(o_ref.dtype)


def _neighbourhood_attention(qkv, bias, heads):
    L = qkv.shape[1]
    nb = L // NA_QBLK
    blk = (heads, NA_QBLK, HEAD_DIM)

    def prev_i(i):
        return jnp.maximum(i - 1, 0)

    def next_i(i):
        return jnp.minimum(i + 1, nb - 1)

    def btype(i):
        return jnp.where(i == 0, 0, jnp.where(i == nb - 1, 2, 1))

    in_specs = [
        pl.BlockSpec(blk, lambda i: (0, i, 0)),
        pl.BlockSpec(blk, lambda i: (1, prev_i(i), 0)),
        pl.BlockSpec(blk, lambda i: (1, i, 0)),
        pl.BlockSpec(blk, lambda i: (1, next_i(i), 0)),
        pl.BlockSpec(blk, lambda i: (2, prev_i(i), 0)),
        pl.BlockSpec(blk, lambda i: (2, i, 0)),
        pl.BlockSpec(blk, lambda i: (2, next_i(i), 0)),
        pl.BlockSpec((1, heads, NA_QBLK, NA_KBLK), lambda i: (btype(i), 0, 0, 0)),
    ]
    return pl.pallas_call(
        _na_kernel, grid=(nb,), in_specs=in_specs,
        out_specs=pl.BlockSpec((NA_QBLK, heads * HEAD_DIM), lambda i: (i, 0)),
        out_shape=jax.ShapeDtypeStruct((L, heads * HEAD_DIM), BF16),
        scratch_shapes=[pltpu.VMEM((heads, NA_KBLK, HEAD_DIM), BF16),
                        pltpu.VMEM((heads, NA_KBLK, 2 * HEAD_DIM), BF16)],
        compiler_params=_params("arbitrary"), name="neighbourhood_attention",
    )(qkv, qkv, qkv, qkv, qkv, qkv, qkv, bias)


def _softplus(z):
    return jnp.maximum(z, 0.0) + jnp.log1p(jnp.exp(-jnp.abs(z)))


def _phase_scan(a_phases, b_phases, order):
    prods, hs = [None] * SUBLANES, [None] * SUBLANES
    p = h = None
    for n, j in enumerate(order):
        if n == 0:
            p, h = a_phases[j], b_phases[j]
        else:
            h = a_phases[j] * h + b_phases[j]
            p = a_phases[j] * p
        prods[j], hs[j] = p, h
    return prods, hs


def _group_carries(pt, ht, carry, scratch, reverse):
    m = pt.shape[0]
    order = list(range(SUBLANES))[::-1] if reverse else list(range(SUBLANES))
    if m == SUBLANES:
        sub = lax.broadcasted_iota(jnp.int32, pt.shape, 0)
        a, b = pt, ht
        for k in (1, 2, 4):
            shift = SUBLANES - k if reverse else k
            a_sh = pltpu.roll(a, shift, 0)
            b_sh = pltpu.roll(b, shift, 0)
            valid = (sub + k < SUBLANES) if reverse else (sub >= k)
            b = jnp.where(valid, a * b_sh + b, b)
            a = jnp.where(valid, a * a_sh, a)
        after = a * carry + b
        edge = SUBLANES - 1 if reverse else 0
        before = jnp.where(sub == edge, carry, pltpu.roll(after, SUBLANES - 1 if reverse else 1, 0))
        last = 0 if reverse else SUBLANES - 1
        return before, jnp.broadcast_to(after[last:last + 1, :], carry.shape)

    pt_ref, ht_ref, cin_ref = scratch[0]
    pt_ref[...] = pt
    ht_ref[...] = ht
    sub_m = m // SUBLANES
    a_ph = [pt_ref[pl.ds(j, sub_m, stride=SUBLANES), :] for j in range(SUBLANES)]
    b_ph = [ht_ref[pl.ds(j, sub_m, stride=SUBLANES), :] for j in range(SUBLANES)]
    prods, hs = _phase_scan(a_ph, b_ph, order)
    inner, carry_out = _group_carries(prods[order[-1]], hs[order[-1]], carry, scratch[1:], reverse)
    for n, j in enumerate(order):
        if n == 0:
            c = inner
        else:
            prev = order[n - 1]
            c = prods[prev] * inner + hs[prev]
        cin_ref[pl.ds(j, sub_m, stride=SUBLANES), :] = c
    return cin_ref[...], carry_out


def _lru_kernel(*refs, reverse):
    if reverse:
        (xc_ref, wg_ref, br_ref, bi_ref, lam_ref, hf_ref, gr_ref, o_ref,
         carry_ref, xs_ref, p_ref, h_ref, *levels) = refs
    else:
        (x_ref, xp_ref, xn_ref, wc_ref, bc_ref, wg_ref, br_ref, bi_ref, lam_ref, o_ref, xc_ref,
         carry_ref, xs_ref, p_ref, h_ref, *levels) = refs
    scratch = [tuple(levels[i:i + 3]) for i in range(0, len(levels), 3)]
    tt = xc_ref.shape[0]
    groups = tt // SUBLANES
    step = pl.program_id(1)
    nt = pl.num_programs(1)
    ti = nt - 1 - step if reverse else step
    order = list(range(SUBLANES))[::-1] if reverse else list(range(SUBLANES))

    if reverse:
        xc = xc_ref[...]
    else:
        xs_ref[0:SUBLANES, :] = jnp.where(ti == 0, 0.0, xp_ref[...])
        xs_ref[SUBLANES:SUBLANES + tt, :] = x_ref[...]
        xs_ref[SUBLANES + tt:, :] = jnp.where(ti == nt - 1, 0.0, xn_ref[...])

        xph = [xs_ref[pl.ds(SUBLANES + j, groups, stride=SUBLANES), :] for j in range(-2, SUBLANES + 1)]
        w = wc_ref[...]
        xc = jnp.concatenate(
            [xph[j] * w[0:1] + xph[j + 1] * w[1:2] + xph[j + 2] * w[2:3] + xph[j + 3] * w[3:4]
             for j in range(SUBLANES)], axis=0) + bc_ref[...]
        xc_ref[...] = xc

    g = jnp.dot(xc.astype(BF16), wg_ref[...], preferred_element_type=F32)
    r = _sigmoid(g[:, :LANES] + br_ref[...])
    ig = _sigmoid(g[:, LANES:] + bi_ref[...])

    log_a = r * (-LRU_C * _softplus(-lam_ref[...]))
    a = jnp.exp(log_a)
    th = jnp.tanh(log_a)
    n = -2.0 * th
    mult = n * lax.rsqrt(jnp.maximum(n * (1.0 - th), F32_TINY))
    u = ig * xc
    b = mult * u

    sub = lax.broadcasted_iota(jnp.int32, (SUBLANES, LANES), 0)
    if reverse:
        lo = tt - SUBLANES
        fix = (sub == SUBLANES - 1) & (ti == nt - 1)
        b = jnp.concatenate([b[:lo], jnp.where(fix, u[lo:], b[lo:])], axis=0)
    else:
        fix = (sub == 0) & (ti == 0)
        b = jnp.concatenate([jnp.where(fix, u[:SUBLANES], b[:SUBLANES]), b[SUBLANES:]], axis=0)

    @pl.when(step == 0)
    def _():
        carry_ref[...] = jnp.zeros_like(carry_ref)

    a_ph = [a[j * groups:(j + 1) * groups] for j in range(SUBLANES)]
    b_ph = [b[j * groups:(j + 1) * groups] for j in range(SUBLANES)]
    prods, hs = _phase_scan(a_ph, b_ph, order)
    for j in range(SUBLANES):
        p_ref[j * groups:(j + 1) * groups, :] = prods[j]
        h_ref[j * groups:(j + 1) * groups, :] = hs[j]
    cin, carry_out = _group_carries(prods[order[-1]], hs[order[-1]], carry_ref[...], scratch, reverse)
    carry_ref[...] = carry_out

    dst = xs_ref if reverse else o_ref
    for j in range(SUBLANES):
        rows = slice(j * groups, (j + 1) * groups)
        dst[pl.ds(j, groups, stride=SUBLANES), :] = p_ref[rows, :] * cin + h_ref[rows, :]
    if reverse:
        h = hf_ref[...] + xs_ref[0:tt, :]
        o_ref[...] = (h * jax.nn.gelu(gr_ref[...])).astype(o_ref.dtype)


def _rg_lru(xg, w_conv, b_conv, w_gates, b_r, b_i, lam, width):
    L = xg.shape[0]
    tt = SUBLANES
    while tt * SUBLANES <= 4096 and L % (tt * SUBLANES) == 0:
        tt *= SUBLANES
    assert tt >= SUBLANES * SUBLANES and width % LANES == 0
    nt = L // tt
    nc = width // LANES
    tpb = tt // SUBLANES
    nhalo = L // SUBLANES

    scratch_shapes = [pltpu.VMEM((SUBLANES, LANES), F32), pltpu.VMEM((tt + 2 * SUBLANES, LANES), F32),
                      pltpu.VMEM((tt, LANES), F32), pltpu.VMEM((tt, LANES), F32)]
    m = tt // SUBLANES
    while m > SUBLANES:
        scratch_shapes += [pltpu.VMEM((m, LANES), F32)] * 3
        m //= SUBLANES

    def run(reverse, head_in, head_specs, tail_in, tail_specs, out_dtypes):
        d = 1 if reverse else 0
        tix = (lambda i: nt - 1 - i) if reverse else (lambda i: i)
        tile = pl.BlockSpec((tt, LANES), lambda c, i: (tix(i), c))
        vec = pl.BlockSpec((1, LANES), lambda c, i: (0, c))
        in_specs = head_specs(tix, tile, vec) + [
            pl.BlockSpec((None, None, LANES, 2 * LANES), lambda c, i: (d, c, 0, 0)), vec, vec, vec,
        ] + tail_specs(tix, tile)
        out = pl.pallas_call(
            functools.partial(_lru_kernel, reverse=reverse),
            grid=(nc, nt), in_specs=in_specs,
            out_specs=[tile] * len(out_dtypes),
            out_shape=[jax.ShapeDtypeStruct((L, width), dt) for dt in out_dtypes],
            scratch_shapes=scratch_shapes,
            compiler_params=_params("parallel", "arbitrary"),
            name="rg_lru_rev" if reverse else "rg_lru_fwd",
        )(*head_in, w_gates, b_r[d].reshape(1, width), b_i[d].reshape(1, width), lam[d].reshape(1, width),
          *tail_in)
        return out

    h_fwd, xc = run(
        False, [xg, xg, xg, w_conv, b_conv.reshape(1, width)],
        lambda tix, tile, vec: [
            tile,
            pl.BlockSpec((SUBLANES, LANES), lambda c, i: (jnp.maximum(tix(i) * tpb - 1, 0), c)),
            pl.BlockSpec((SUBLANES, LANES), lambda c, i: (jnp.minimum((tix(i) + 1) * tpb, nhalo - 1), c)),
            pl.BlockSpec((CONV_W, LANES), lambda c, i: (0, c)),
            vec],
        [], lambda tix, tile: [], [F32, F32])
    (lru,) = run(
        True, [xc], lambda tix, tile, vec: [tile],
        [h_fwd, xg], lambda tix, tile: [tile, pl.BlockSpec((tt, LANES), lambda c, i: (tix(i), c + nc))],
        [BF16])
    return lru


def _prepare_weights(w_in, w_rgate, w_igate, w_na_out, w_lru_out, w_merge, w_out,
                     w_ffn_gate, w_ffn_up, w_ffn_down):
    w_gates = jnp.concatenate([w_rgate, w_igate], axis=-1).astype(BF16)
    return dict(
        w_in=w_in.astype(BF16),
        w_merge=w_merge.astype(BF16),
        w_gates=w_gates,
        w_na_out=w_na_out.astype(BF16),
        w_lru_out=w_lru_out.astype(BF16),
        w_out=w_out.astype(BF16),
        w_ffn_gate=w_ffn_gate.astype(BF16),
        w_ffn_up=w_ffn_up.astype(BF16),
        w_ffn_down=w_ffn_down.astype(BF16),
    )


def _encoder_layer(x, w, bias_cache, w_conv, b_conv, b_rgate, b_igate, lru_lambda, rpb, b_merge,
                   g_mix_pre, g_mix_post, g_ffn_pre, g_ffn_post):
    L, D = x.shape
    heads = w["w_na_out"].shape[0] // HEAD_DIM
    lru_width = w["w_lru_out"].shape[0]
    rows = L // GRID_W
    assert L % NA_QBLK == 0 and rows >= max(WIN_ROWS, 3 * NA_ROWS_PER_BLOCK)

    xn = _rmsnorm(x, g_mix_pre, BF16)
    na_width = heads * HEAD_DIM
    bn_in = _tile(math.gcd(na_width, lru_width), 1024)
    qkv = _mm(xn, w["w_in"], cols=(0, 3 * na_width), out_dtype=BF16, bm=1024, bn=bn_in, head_major=True,
              scaled_cols=na_width, scale=HEAD_DIM ** -0.5)
    xg = _mm(xn, w["w_in"], cols=(3 * na_width, 2 * lru_width), out_dtype=F32, bm=1024, bn=bn_in)
    gates = _mm(xn, w["w_merge"], out_dtype=BF16, bm=1024, bn=1024, bias=b_merge, act="sigmoid")

    na = _neighbourhood_attention(qkv, _na_bias_tables(rpb, rows, bias_cache), heads)
    lru = _rg_lru(xg, w_conv, b_conv, w["w_gates"], b_rgate, b_igate, lru_lambda, lru_width)

    m = _gated_out(na, lru, w["w_na_out"], w["w_lru_out"], gates, bm=1024, bn=1024)
    x1, x1n = _out_residual(m, w["w_out"], x, g_mix_post, g_ffn_pre, bm=256, parts=2)

    h = _swiglu_up(x1n, w["w_ffn_gate"], w["w_ffn_up"], bm=1024, bn=512)
    f = _mm(h, w["w_ffn_down"], out_dtype=BF16, bm=512, bn=512)
    return _residual(x1, f, g_ffn_post)


def _per_batch(y, args):
    if y.shape[0] == 1:
        return _encoder_layer(y[0], *args)[None]
    return jnp.stack([_encoder_layer(y[b], *args) for b in range(y.shape[0])])


def kernel(x_prompt, x_sample, w_in, w_conv, b_conv, w_rgate, b_rgate, w_igate, b_igate, lru_lambda,
           rpb, w_na_out, w_lru_out, w_merge, b_merge, w_out, g_mix_pre, g_mix_post, g_ffn_pre,
           g_ffn_post, w_ffn_gate, w_ffn_up, w_ffn_down):
    y_prompt, y_sample = x_prompt, x_sample
    for l in range(w_in.shape[0]):
        w = _prepare_weights(w_in[l], w_rgate[l], w_igate[l], w_na_out[l], w_lru_out[l], w_merge[l],
                             w_out[l], w_ffn_gate[l], w_ffn_up[l], w_ffn_down[l])
        args = (w, {}, w_conv[l], b_conv[l], b_rgate[l], b_igate[l], lru_lambda[l], rpb[l], b_merge[l],
                g_mix_pre[l], g_mix_post[l], g_ffn_pre[l], g_ffn_post[l])
        y_prompt = _per_batch(y_prompt, args)
        y_sample = _per_batch(y_sample, args)
    return (y_prompt, y_sample)
```

```python
import functools
import math

import jax
import jax.numpy as jnp
import numpy as np
from jax import lax
from jax.experimental import pallas as pl
from jax.experimental.pallas import tpu as pltpu

F32 = jnp.float32
BF16 = jnp.bfloat16

GRID_W = 64
HEAD_DIM = 128
WIN_ROWS = 8
WIN_COLS = 16
CONV_W = 4
LRU_C = 8.0
EPS = 1e-6

LANES = 128
SUBLANES = 8
VMEM_BYTES = 64 * 1024 * 1024
VMEM_LIMIT_BYTES = 56 * 1024 * 1024

NA_ROWS_PER_BLOCK = 4
NA_QBLK = NA_ROWS_PER_BLOCK * GRID_W
NA_KBLK = 3 * NA_QBLK
MASK_VALUE = -1e30
F32_TINY = float(np.finfo(np.float32).tiny)


def _params(*semantics, vmem_limit_bytes=VMEM_LIMIT_BYTES):
    return pltpu.CompilerParams(dimension_semantics=semantics, vmem_limit_bytes=vmem_limit_bytes)


def _tile(n, pref):
    if n <= pref:
        return n
    t = (pref // LANES) * LANES
    while t >= LANES:
        if n % t == 0:
            return t
        t -= LANES
    return n


def _rms(x):
    return x * lax.rsqrt(jnp.mean(x * x, axis=-1, keepdims=True) + EPS)


def _sigmoid(z):
    return 0.5 * jnp.tanh(0.5 * z) + 0.5


def _rmsnorm_kernel(x_ref, g_ref, o_ref):
    o_ref[...] = (_rms(x_ref[...]) * g_ref[...]).astype(o_ref.dtype)


def _rmsnorm(x, g, out_dtype):
    L, D = x.shape
    bm = _tile(L, 256)
    return pl.pallas_call(
        _rmsnorm_kernel,
        grid=(L // bm,),
        in_specs=[pl.BlockSpec((bm, D), lambda i: (i, 0)), pl.BlockSpec((1, D), lambda i: (0, 0))],
        out_specs=pl.BlockSpec((bm, D), lambda i: (i, 0)),
        out_shape=jax.ShapeDtypeStruct((L, D), out_dtype),
        compiler_params=_params("parallel"),
        name="rmsnorm",
    )(x, g.reshape(1, D))


def _residual_kernel(x_ref, y_ref, g_post_ref, o_ref):
    o_ref[...] = x_ref[...] + _rms(y_ref[...].astype(F32)) * g_post_ref[...]


def _residual(x, y, g_post):
    L, D = x.shape
    bm = _tile(L, 256)
    row = pl.BlockSpec((bm, D), lambda i: (i, 0))
    return pl.pallas_call(
        _residual_kernel, grid=(L // bm,), in_specs=[row, row, pl.BlockSpec((1, D), lambda i: (0, 0))],
        out_specs=row, out_shape=jax.ShapeDtypeStruct((L, D), F32), compiler_params=_params("parallel"),
        name="residual",
    )(x, y, g_post.reshape(1, D))


def _out_residual_kernel(m_ref, w_ref, x_ref, g_post_ref, g_next_ref, o_ref, on_ref, *, parts):
    rows = m_ref.shape[0] // parts
    for p in range(parts):
        sl = slice(p * rows, (p + 1) * rows)
        mix = jnp.dot(m_ref[sl, :], w_ref[...], preferred_element_type=F32)
        o = x_ref[sl, :] + _rms(mix) * g_post_ref[...]
        o_ref[sl, :] = o
        on_ref[sl, :] = (_rms(o) * g_next_ref[...]).astype(on_ref.dtype)


def _out_residual(m, w, x, g_post, g_next, *, bm, parts):
    L, D = x.shape
    K = m.shape[1]
    bm = _tile(L, bm)
    row = lambda width: pl.BlockSpec((bm, width), lambda i: (i, 0))
    vec = pl.BlockSpec((1, D), lambda i: (0, 0))
    need = 2 * K * D + 2 * bm * (2 * K + 10 * D) + 4 * (bm // parts) * D
    limit = min(VMEM_BYTES - 4 * 1024 * 1024, max(VMEM_LIMIT_BYTES, need + 2 * 1024 * 1024))
    return pl.pallas_call(
        functools.partial(_out_residual_kernel, parts=parts), grid=(L // bm,),
        in_specs=[row(K), pl.BlockSpec((K, D), lambda i: (0, 0), pipeline_mode=pl.Buffered(1)),
                  row(D), vec, vec],
        out_specs=[row(D), row(D)],
        out_shape=[jax.ShapeDtypeStruct((L, D), F32), jax.ShapeDtypeStruct((L, D), BF16)],
        compiler_params=_params("parallel", vmem_limit_bytes=limit), name="out_residual",
    )(m, w, x, g_post.reshape(1, D), g_next.reshape(1, D))


def _mm_kernel(a_ref, b_ref, *rest, act, has_bias, head_major, scaled_tiles, scale):
    if has_bias:
        bias_ref, o_ref = rest
    else:
        (o_ref,) = rest
    acc = jnp.dot(a_ref[...], b_ref[...], preferred_element_type=F32)
    if has_bias:
        acc = acc + bias_ref[...]
    if scaled_tiles:
        acc = acc * jnp.where(pl.program_id(1) < scaled_tiles, scale, 1.0)
    if act == "sigmoid":
        acc = _sigmoid(acc)
    if head_major:
        for c in range(o_ref.shape[0]):
            o_ref[c] = acc[:, c * LANES:(c + 1) * LANES].astype(o_ref.dtype)
    else:
        o_ref[...] = acc.astype(o_ref.dtype)


def _mm(a, b, *, out_dtype, bm, bn, bias=None, act=None, head_major=False, scaled_cols=0, scale=1.0,
        cols=None):
    M, K = a.shape
    col0, N = cols if cols is not None else (0, b.shape[1])
    bm = _tile(M, bm)
    bn = _tile(N, bn)
    assert scaled_cols % bn == 0 and col0 % bn == 0
    j0 = col0 // bn
    in_specs = [pl.BlockSpec((bm, K), lambda i, j: (i, 0)), pl.BlockSpec((K, bn), lambda i, j: (0, j + j0))]
    args = [a, b]
    if bias is not None:
        in_specs.append(pl.BlockSpec((1, bn), lambda i, j: (0, j)))
        args.append(bias.reshape(1, N).astype(F32))
    if head_major:
        out_specs = pl.BlockSpec((bn // LANES, bm, LANES), lambda i, j: (j, i, 0))
        out_shape = jax.ShapeDtypeStruct((N // LANES, M, LANES), out_dtype)
    else:
        out_specs = pl.BlockSpec((bm, bn), lambda i, j: (i, j))
        out_shape = jax.ShapeDtypeStruct((M, N), out_dtype)
    return pl.pallas_call(
        functools.partial(_mm_kernel, act=act, has_bias=bias is not None, head_major=head_major,
                          scaled_tiles=scaled_cols // bn, scale=scale),
        grid=(M // bm, N // bn), in_specs=in_specs, out_specs=out_specs, out_shape=out_shape,
        compiler_params=_params("parallel", "arbitrary"), name="matmul",
    )(*args)


def _gated_out_kernel(na_ref, lru_ref, wna_ref, wlru_ref, gna_ref, glru_ref, o_ref):
    o_na = jnp.dot(na_ref[...], wna_ref[...], preferred_element_type=F32)
    o_lru = jnp.dot(lru_ref[...], wlru_ref[...], preferred_element_type=F32)
    o_ref[...] = (gna_ref[...].astype(F32) * o_na + glru_ref[...].astype(F32) * o_lru).astype(o_ref.dtype)


def _gated_out(na, lru, w_na, w_lru, gates, *, bm, bn):
    M, K = na.shape
    N = w_na.shape[1]
    bm = _tile(M, bm)
    bn = _tile(N, bn)
    nj = N // bn
    act = pl.BlockSpec((bm, K), lambda i, j: (i, 0))
    wgt = pl.BlockSpec((K, bn), lambda i, j: (0, j))
    return pl.pallas_call(
        _gated_out_kernel, grid=(M // bm, nj),
        in_specs=[act, act, wgt, wgt,
                  pl.BlockSpec((bm, bn), lambda i, j: (i, j)),
                  pl.BlockSpec((bm, bn), lambda i, j: (i, j + nj))],
        out_specs=pl.BlockSpec((bm, bn), lambda i, j: (i, j)),
        out_shape=jax.ShapeDtypeStruct((M, N), BF16),
        compiler_params=_params("parallel", "arbitrary"), name="gated_out",
    )(na, lru, w_na, w_lru, gates, gates)


def _swiglu_up_kernel(a_ref, wg_ref, wu_ref, o_ref):
    a = a_ref[...]
    g = jnp.dot(a, wg_ref[...], preferred_element_type=F32)
    u = jnp.dot(a, wu_ref[...], preferred_element_type=F32)
    o_ref[...] = (g * _sigmoid(g) * u).astype(o_ref.dtype)


def _swiglu_up(a, w_gate, w_up, *, bm, bn):
    M, K = a.shape
    N = w_gate.shape[1]
    bm = _tile(M, bm)
    bn = min(bn, N)
    wgt = pl.BlockSpec((K, bn), lambda i, j: (0, j))
    return pl.pallas_call(
        _swiglu_up_kernel, grid=(M // bm, pl.cdiv(N, bn)),
        in_specs=[pl.BlockSpec((bm, K), lambda i, j: (i, 0)), wgt, wgt],
        out_specs=pl.BlockSpec((bm, bn), lambda i, j: (i, j)),
        out_shape=jax.ShapeDtypeStruct((M, N), BF16),
        compiler_params=_params("parallel", "arbitrary"), name="swiglu_up",
    )(a, w_gate, w_up)


def _na_bias_tables(rpb, rows, cache):
    heads = rpb.shape[0]
    rb = NA_ROWS_PER_BLOCK
    nb = rows // rb

    def valid(i):
        qr = i * rb + np.arange(rb)[:, None, None, None]
        qc = np.arange(GRID_W)[None, :, None, None]
        kr = (i - 1) * rb + np.arange(3 * rb)[None, None, :, None]
        kc = np.arange(GRID_W)[None, None, None, :]
        rs = np.clip(qr - WIN_ROWS // 2, 0, rows - WIN_ROWS)
        cs = np.clip(qc - WIN_COLS // 2, 0, GRID_W - WIN_COLS)
        return (kr >= rs) & (kr < rs + WIN_ROWS) & (kc >= cs) & (kc < cs + WIN_COLS)

    masks = np.stack([valid(i) for i in (0, 1, nb - 1)])
    key = masks.tobytes()
    if key in cache:
        return cache[key]
    dc = np.arange(GRID_W)[None, :] - np.arange(GRID_W)[:, None] + (WIN_COLS - 1)
    t = sum(jnp.where(dc == d, rpb[:, :, d][:, :, None, None].astype(F32), 0.0)
            for d in range(2 * WIN_COLS - 1))
    off = WIN_ROWS - 1 - rb
    tb = jnp.stack([jnp.swapaxes(t[:, off - a:off - a + 3 * rb], 1, 2) for a in range(rb)], axis=1)
    tables = jnp.where(masks[:, None], tb[None], MASK_VALUE).reshape(3, heads, NA_QBLK, NA_KBLK)
    cache[key] = tables
    return tables


def _na_kernel(q_ref, kp_ref, kc_ref, kn_ref, vp_ref, vc_ref, vn_ref, b_ref, o_ref, kcat_ref, vext_ref):
    @pl.when(pl.program_id(0) == 0)
    def _():
        vext_ref[:, :, HEAD_DIM:] = jnp.ones(vext_ref.shape[:2] + (HEAD_DIM,), BF16)

    for h in range(q_ref.shape[0]):
        for seg, (k_ref, v_ref) in enumerate(((kp_ref, vp_ref), (kc_ref, vc_ref), (kn_ref, vn_ref))):
            rows = slice(seg * NA_QBLK, (seg + 1) * NA_QBLK)
            kcat_ref[h, rows, :] = k_ref[h]
            vext_ref[h, rows, :HEAD_DIM] = v_ref[h]
        s = lax.dot_general(q_ref[h], kcat_ref[h], (((1,), (1,)), ((), ())),
                            preferred_element_type=F32) + b_ref[0, h]
        m = jnp.max(s, axis=-1, keepdims=True)
        p = jnp.exp(s - m).astype(BF16)
        oe = jnp.dot(p, vext_ref[h], preferred_element_type=F32)
        o_ref[:, h * HEAD_DIM:(h + 1) * HEAD_DIM] = (oe[:, :HEAD_DIM] / oe[:, HEAD_DIM:]).astype(o_ref.dtype)


def _neighbourhood_attention(qkv, bias, heads):
    L = qkv.shape[1]
    nb = L // NA_QBLK
    blk = (heads, NA_QBLK, HEAD_DIM)

    def prev_i(i):
        return jnp.maximum(i - 1, 0)

    def next_i(i):
        return jnp.minimum(i + 1, nb - 1)

    def btype(i):
        return jnp.where(i == 0, 0, jnp.where(i == nb - 1, 2, 1))

    in_specs = [
        pl.BlockSpec(blk, lambda i: (0, i, 0)),
        pl.BlockSpec(blk, lambda i: (1, prev_i(i), 0)),
        pl.BlockSpec(blk, lambda i: (1, i, 0)),
        pl.BlockSpec(blk, lambda i: (1, next_i(i), 0)),
        pl.BlockSpec(blk, lambda i: (2, prev_i(i), 0)),
        pl.BlockSpec(blk, lambda i: (2, i, 0)),
        pl.BlockSpec(blk, lambda i: (2, next_i(i), 0)),
        pl.BlockSpec((1, heads, NA_QBLK, NA_KBLK), lambda i: (btype(i), 0, 0, 0)),
    ]
    return pl.pallas_call(
        _na_kernel, grid=(nb,), in_specs=in_specs,
        out_specs=pl.BlockSpec((NA_QBLK, heads * HEAD_DIM), lambda i: (i, 0)),
        out_shape=jax.ShapeDtypeStruct((L, heads * HEAD_DIM), BF16),
        scratch_shapes=[pltpu.VMEM((heads, NA_KBLK, HEAD_DIM), BF16),
                        pltpu.VMEM((heads, NA_KBLK, 2 * HEAD_DIM), BF16)],
        compiler_params=_params("arbitrary"), name="neighbourhood_attention",
    )(qkv, qkv, qkv, qkv, qkv, qkv, qkv, bias)


def _softplus(z):
    return jnp.maximum(z, 0.0) + jnp.log1p(jnp.exp(-jnp.abs(z)))


def _phase_scan(a_phases, b_phases, order):
    prods, hs = [None] * SUBLANES, [None] * SUBLANES
    p = h = None
    for n, j in enumerate(order):
        if n == 0:
            p, h = a_phases[j], b_phases[j]
        else:
            h = a_phases[j] * h + b_phases[j]
            p = a_phases[j] * p
        prods[j], hs[j] = p, h
    return prods, hs


def _group_carries(pt, ht, carry, scratch, reverse):
    m = pt.shape[0]
    order = list(range(SUBLANES))[::-1] if reverse else list(range(SUBLANES))
    if m == SUBLANES:
        sub = lax.broadcasted_iota(jnp.int32, pt.shape, 0)
        a, b = pt, ht
        for k in (1, 2, 4):
            shift = SUBLANES - k if reverse else k
            a_sh = pltpu.roll(a, shift, 0)
            b_sh = pltpu.roll(b, shift, 0)
            valid = (sub + k < SUBLANES) if reverse else (sub >= k)
            b = jnp.where(valid, a * b_sh + b, b)
            a = jnp.where(valid, a * a_sh, a)
        after = a * carry + b
        edge = SUBLANES - 1 if reverse else 0
        before = jnp.where(sub == edge, carry, pltpu.roll(after, SUBLANES - 1 if reverse else 1, 0))
        last = 0 if reverse else SUBLANES - 1
        return before, jnp.broadcast_to(after[last:last + 1, :], carry.shape)

    pt_ref, ht_ref, cin_ref = scratch[0]
    pt_ref[...] = pt
    ht_ref[...] = ht
    sub_m = m // SUBLANES
    a_ph = [pt_ref[pl.ds(j, sub_m, stride=SUBLANES), :] for j in range(SUBLANES)]
    b_ph = [ht_ref[pl.ds(j, sub_m, stride=SUBLANES), :] for j in range(SUBLANES)]
    prods, hs = _phase_scan(a_ph, b_ph, order)
    inner, carry_out = _group_carries(prods[order[-1]], hs[order[-1]], carry, scratch[1:], reverse)
    for n, j in enumerate(order):
        if n == 0:
            c = inner
        else:
            prev = order[n - 1]
            c = prods[prev] * inner + hs[prev]
        cin_ref[pl.ds(j, sub_m, stride=SUBLANES), :] = c
    return cin_ref[...], carry_out


def _lru_kernel(*refs, reverse):
    if reverse:
        (xc_ref, wg_ref, br_ref, bi_ref, lam_ref, hf_ref, gr_ref, o_ref,
         carry_ref, xs_ref, p_ref, h_ref, *levels) = refs
    else:
        (x_ref, xp_ref, xn_ref, wc_ref, bc_ref, wg_ref, br_ref, bi_ref, lam_ref, o_ref, xc_ref,
         carry_ref, xs_ref, p_ref, h_ref, *levels) = refs
    scratch = [tuple(levels[i:i + 3]) for i in range(0, len(levels), 3)]
    tt = xc_ref.shape[0]
    groups = tt // SUBLANES
    step = pl.program_id(1)
    nt = pl.num_programs(1)
    ti = nt - 1 - step if reverse else step
    order = list(range(SUBLANES))[::-1] if reverse else list(range(SUBLANES))

    if reverse:
        xc = xc_ref[...]
    else:
        xs_ref[0:SUBLANES, :] = jnp.where(ti == 0, 0.0, xp_ref[...])
        xs_ref[SUBLANES:SUBLANES + tt, :] = x_ref[...]
        xs_ref[SUBLANES + tt:, :] = jnp.where(ti == nt - 1, 0.0, xn_ref[...])

        xph = [xs_ref[pl.ds(SUBLANES + j, groups, stride=SUBLANES), :] for j in range(-2, SUBLANES + 1)]
        w = wc_ref[...]
        xc = jnp.concatenate(
            [xph[j] * w[0:1] + xph[j + 1] * w[1:2] + xph[j + 2] * w[2:3] + xph[j + 3] * w[3:4]
             for j in range(SUBLANES)], axis=0) + bc_ref[...]
        xc_ref[...] = xc

    g = jnp.dot(xc.astype(BF16), wg_ref[...], preferred_element_type=F32)
    r = _sigmoid(g[:, :LANES] + br_ref[...])
    ig = _sigmoid(g[:, LANES:] + bi_ref[...])

    log_a = r * (-LRU_C * _softplus(-lam_ref[...]))
    a = jnp.exp(log_a)
    th = jnp.tanh(log_a)
    n = -2.0 * th
    mult = n * lax.rsqrt(jnp.maximum(n * (1.0 - th), F32_TINY))
    u = ig * xc
    b = mult * u

    sub = lax.broadcasted_iota(jnp.int32, (SUBLANES, LANES), 0)
    if reverse:
        lo = tt - SUBLANES
        fix = (sub == SUBLANES - 1) & (ti == nt - 1)
        b = jnp.concatenate([b[:lo], jnp.where(fix, u[lo:], b[lo:])], axis=0)
    else:
        fix = (sub == 0) & (ti == 0)
        b = jnp.concatenate([jnp.where(fix, u[:SUBLANES], b[:SUBLANES]), b[SUBLANES:]], axis=0)

    @pl.when(step == 0)
    def _():
        carry_ref[...] = jnp.zeros_like(carry_ref)

    a_ph = [a[j * groups:(j + 1) * groups] for j in range(SUBLANES)]
    b_ph = [b[j * groups:(j + 1) * groups] for j in range(SUBLANES)]
    prods, hs = _phase_scan(a_ph, b_ph, order)
    for j in range(SUBLANES):
        p_ref[j * groups:(j + 1) * groups, :] = prods[j]
        h_ref[j * groups:(j + 1) * groups, :] = hs[j]
    cin, carry_out = _group_carries(prods[order[-1]], hs[order[-1]], carry_ref[...], scratch, reverse)
    carry_ref[...] = carry_out

    dst = xs_ref if reverse else o_ref
    for j in range(SUBLANES):
        rows = slice(j * groups, (j + 1) * groups)
        dst[pl.ds(j, groups, stride=SUBLANES), :] = p_ref[rows, :] * cin + h_ref[rows, :]
    if reverse:
        h = hf_ref[...] + xs_ref[0:tt, :]
        o_ref[...] = (h * jax.nn.gelu(gr_ref[...])).astype(o_ref.dtype)


def _rg_lru(xg, w_conv, b_conv, w_gates, b_r, b_i, lam, width):
    L = xg.shape[0]
    tt = SUBLANES
    while tt * SUBLANES <= 4096 and L % (tt * SUBLANES) == 0:
        tt *= SUBLANES
    assert tt >= SUBLANES * SUBLANES and width % LANES == 0
    nt = L // tt
    nc = width // LANES
    tpb = tt // SUBLANES
    nhalo = L // SUBLANES

    scratch_shapes = [pltpu.VMEM((SUBLANES, LANES), F32), pltpu.VMEM((tt + 2 * SUBLANES, LANES), F32),
                      pltpu.VMEM((tt, LANES), F32), pltpu.VMEM((tt, LANES), F32)]
    m = tt // SUBLANES
    while m > SUBLANES:
        scratch_shapes += [pltpu.VMEM((m, LANES), F32)] * 3
        m //= SUBLANES

    def run(reverse, head_in, head_specs, tail_in, tail_specs, out_dtypes):
        d = 1 if reverse else 0
        tix = (lambda i: nt - 1 - i) if reverse else (lambda i: i)
        tile = pl.BlockSpec((tt, LANES), lambda c, i: (tix(i), c))
        vec = pl.BlockSpec((1, LANES), lambda c, i: (0, c))
        in_specs = head_specs(tix, tile, vec) + [
            pl.BlockSpec((None, None, LANES, 2 * LANES), lambda c, i: (d, c, 0, 0)), vec, vec, vec,
        ] + tail_specs(tix, tile)
        out = pl.pallas_call(
            functools.partial(_lru_kernel, reverse=reverse),
            grid=(nc, nt), in_specs=in_specs,
            out_specs=[tile] * len(out_dtypes),
            out_shape=[jax.ShapeDtypeStruct((L, width), dt) for dt in out_dtypes],
            scratch_shapes=scratch_shapes,
            compiler_params=_params("parallel", "arbitrary"),
            name="rg_lru_rev" if reverse else "rg_lru_fwd",
        )(*head_in, w_gates, b_r[d].reshape(1, width), b_i[d].reshape(1, width), lam[d].reshape(1, width),
          *tail_in)
        return out

    h_fwd, xc = run(
        False, [xg, xg, xg, w_conv, b_conv.reshape(1, width)],
        lambda tix, tile, vec: [
            tile,
            pl.BlockSpec((SUBLANES, LANES), lambda c, i: (jnp.maximum(tix(i) * tpb - 1, 0), c)),
            pl.BlockSpec((SUBLANES, LANES), lambda c, i: (jnp.minimum((tix(i) + 1) * tpb, nhalo - 1), c)),
            pl.BlockSpec((CONV_W, LANES), lambda c, i: (0, c)),
            vec],
        [], lambda tix, tile: [], [F32, F32])
    (lru,) = run(
        True, [xc], lambda tix, tile, vec: [tile],
        [h_fwd, xg], lambda tix, tile: [tile, pl.BlockSpec((tt, LANES), lambda c, i: (tix(i), c + nc))],
        [BF16])
    return lru


def _prepare_weights(w_in, w_rgate, w_igate, w_na_out, w_lru_out, w_merge, w_out,
                     w_ffn_gate, w_ffn_up, w_ffn_down):
    w_gates = jnp.concatenate([w_rgate, w_igate], axis=-1).astype(BF16)
    return dict(
        w_in=w_in.astype(BF16),
        w_merge=w_merge.astype(BF16),
        w_gates=w_gates,
        w_na_out=w_na_out.astype(BF16),
        w_lru_out=w_lru_out.astype(BF16),
        w_out=w_out.astype(BF16),
        w_ffn_gate=w_ffn_gate.astype(BF16),
        w_ffn_up=w_ffn_up.astype(BF16),
        w_ffn_down=w_ffn_down.astype(BF16),
    )


def _encoder_layer(x, w, bias_cache, w_conv, b_conv, b_rgate, b_igate, lru_lambda, rpb, b_merge,
                   g_mix_pre, g_mix_post, g_ffn_pre, g_ffn_post):
    L, D = x.shape
    heads = w["w_na_out"].shape[0] // HEAD_DIM
    lru_width = w["w_lru_out"].shape[0]
    rows = L // GRID_W
    assert L % NA_QBLK == 0 and rows >= max(WIN_ROWS, 3 * NA_ROWS_PER_BLOCK)

    xn = _rmsnorm(x, g_mix_pre, BF16)
    na_width = heads * HEAD_DIM
    bn_in = _tile(math.gcd(na_width, lru_width), 1024)
    qkv = _mm(xn, w["w_in"], cols=(0, 3 * na_width), out_dtype=BF16, bm=1024, bn=bn_in, head_major=True,
              scaled_cols=na_width, scale=HEAD_DIM ** -0.5)
    xg = _mm(xn, w["w_in"], cols=(3 * na_width, 2 * lru_width), out_dtype=F32, bm=1024, bn=bn_in)
    gates = _mm(xn, w["w_merge"], out_dtype=BF16, bm=1024, bn=1024, bias=b_merge, act="sigmoid")

    na = _neighbourhood_attention(qkv, _na_bias_tables(rpb, rows, bias_cache), heads)
    lru = _rg_lru(xg, w_conv, b_conv, w["w_gates"], b_rgate, b_igate, lru_lambda, lru_width)

    m = _gated_out(na, lru, w["w_na_out"], w["w_lru_out"], gates, bm=1024, bn=1024)
    x1, x1n = _out_residual(m, w["w_out"], x, g_mix_post, g_ffn_pre, bm=256, parts=2)

    h = _swiglu_up(x1n, w["w_ffn_gate"], w["w_ffn_up"], bm=2048, bn=256)
    f = _mm(h, w["w_ffn_down"], out_dtype=BF16, bm=512, bn=512)
    return _residual(x1, f, g_ffn_post)


def _per_batch(y, args):
    if y.shape[0] == 1:
        return _encoder_layer(y[0], *args)[None]
    return jnp.stack([_encoder_layer(y[b], *args) for b in range(y.shape[0])])


def kernel(x_prompt, x_sample, w_in, w_conv, b_conv, w_rgate, b_rgate, w_igate, b_igate, lru_lambda,
           rpb, w_na_out, w_lru_out, w_merge, b_merge, w_out, g_mix_pre, g_mix_post, g_ffn_pre,
           g_ffn_post, w_ffn_gate, w_ffn_up, w_ffn_down):
    y_prompt, y_sample = x_prompt, x_sample
    for l in range(w_in.shape[0]):
        w = _prepare_weights(w_in[l], w_rgate[l], w_igate[l], w_na_out[l], w_lru_out[l], w_merge[l],
                             w_out[l], w_ffn_gate[l], w_ffn_up[l], w_ffn_down[l])
        args = (w, {}, w_conv[l], b_conv[l], b_rgate[l], b_igate[l], lru_lambda[l], rpb[l], b_merge[l],
                g_mix_pre[l], g_mix_post[l], g_ffn_pre[l], g_ffn_post[l])
        y_prompt = _per_batch(y_prompt, args)
        y_sample = _per_batch(y_sample, args)
    return (y_prompt, y_sample)
```

```python
import functools
import math

import jax
import jax.numpy as jnp
import numpy as np
from jax import lax
from jax.experimental import pallas as pl
from jax.experimental.pallas import tpu as pltpu

F32 = jnp.float32
BF16 = jnp.bfloat16

GRID_W = 64
HEAD_DIM = 128
WIN_ROWS = 8
WIN_COLS = 16
CONV_W = 4
LRU_C = 8.0
EPS = 1e-6

LANES = 128
SUBLANES = 8
VMEM_BYTES = 64 * 1024 * 1024
VMEM_LIMIT_BYTES = 56 * 1024 * 1024

NA_ROWS_PER_BLOCK = 4
NA_QBLK = NA_ROWS_PER_BLOCK * GRID_W
NA_KBLK = 3 * NA_QBLK
MASK_VALUE = -1e30
F32_TINY = float(np.finfo(np.float32).tiny)


def _params(*semantics, vmem_limit_bytes=VMEM_LIMIT_BYTES):
    return pltpu.CompilerParams(dimension_semantics=semantics, vmem_limit_bytes=vmem_limit_bytes)


def _tile(n, pref):
    if n <= pref:
        return n
    t = (pref // LANES) * LANES
    while t >= LANES:
        if n % t == 0:
            return t
        t -= LANES
    return n


def _rms(x):
    return x * lax.rsqrt(jnp.mean(x * x, axis=-1, keepdims=True) + EPS)


def _sigmoid(z):
    return 0.5 * jnp.tanh(0.5 * z) + 0.5


def _segment_bounds(lengths, unit):
    assert all(n % unit == 0 for n in lengths)
    bounds = [0]
    for n in lengths:
        bounds.append(bounds[-1] + n // unit)
    return bounds


def _rmsnorm_kernel(*refs, bounds):
    *x_refs, g_ref, o_ref = refs
    i = pl.program_id(0)
    for k, x_ref in enumerate(x_refs):
        @pl.when((i >= bounds[k]) & (i < bounds[k + 1]))
        def _(x_ref=x_ref):
            o_ref[...] = (_rms(x_ref[...]) * g_ref[...]).astype(o_ref.dtype)


def _rmsnorm(xs, g, out_dtype):
    D = xs[0].shape[1]
    bm = functools.reduce(math.gcd, [x.shape[0] for x in xs] + [256])
    bounds = _segment_bounds([x.shape[0] for x in xs], bm)

    def source(k):
        return pl.BlockSpec((bm, D), lambda i: (jnp.clip(i - bounds[k], 0, bounds[k + 1] - bounds[k] - 1), 0))

    return pl.pallas_call(
        functools.partial(_rmsnorm_kernel, bounds=bounds),
        grid=(bounds[-1],),
        in_specs=[source(k) for k in range(len(xs))] + [pl.BlockSpec((1, D), lambda i: (0, 0))],
        out_specs=pl.BlockSpec((bm, D), lambda i: (i, 0)),
        out_shape=jax.ShapeDtypeStruct((bounds[-1] * bm, D), out_dtype),
        compiler_params=_params("arbitrary"),
        name="rmsnorm",
    )(*xs, g.reshape(1, D))


def _residual_kernel(x_ref, y_ref, g_post_ref, o_ref):
    o_ref[...] = x_ref[...] + _rms(y_ref[...].astype(F32)) * g_post_ref[...]


def _residual(x, y, g_post):
    L, D = x.shape
    bm = _tile(L, 256)
    row = pl.BlockSpec((bm, D), lambda i: (i, 0))
    return pl.pallas_call(
        _residual_kernel, grid=(L // bm,), in_specs=[row, row, pl.BlockSpec((1, D), lambda i: (0, 0))],
        out_specs=row, out_shape=jax.ShapeDtypeStruct((L, D), F32), compiler_params=_params("parallel"),
        name="residual",
    )(x, y, g_post.reshape(1, D))


def _out_residual_kernel(m_ref, w_ref, x_ref, g_post_ref, g_next_ref, o_ref, on_ref, *, parts):
    rows = m_ref.shape[0] // parts
    for p in range(parts):
        sl = slice(p * rows, (p + 1) * rows)
        mix = jnp.dot(m_ref[sl, :], w_ref[...], preferred_element_type=F32)
        o = x_ref[sl, :] + _rms(mix) * g_post_ref[...]
        o_ref[sl, :] = o
        on_ref[sl, :] = (_rms(o) * g_next_ref[...]).astype(on_ref.dtype)


def _out_residual(m, w, x, g_post, g_next, *, bm, parts, m_row0=0):
    L, D = x.shape
    K = m.shape[1]
    bm = math.gcd(_tile(L, bm), m_row0) if m_row0 else _tile(L, bm)
    i0 = m_row0 // bm
    row = lambda width: pl.BlockSpec((bm, width), lambda i: (i, 0))
    vec = pl.BlockSpec((1, D), lambda i: (0, 0))
    need = 2 * K * D + 2 * bm * (2 * K + 10 * D) + 4 * (bm // parts) * D
    limit = min(VMEM_BYTES - 4 * 1024 * 1024, max(VMEM_LIMIT_BYTES, need + 2 * 1024 * 1024))
    return pl.pallas_call(
        functools.partial(_out_residual_kernel, parts=parts), grid=(L // bm,),
        in_specs=[pl.BlockSpec((bm, K), lambda i: (i + i0, 0)),
                  pl.BlockSpec((K, D), lambda i: (0, 0), pipeline_mode=pl.Buffered(1)),
                  row(D), vec, vec],
        out_specs=[row(D), row(D)],
        out_shape=[jax.ShapeDtypeStruct((L, D), F32), jax.ShapeDtypeStruct((L, D), BF16)],
        compiler_params=_params("parallel", vmem_limit_bytes=limit), name="out_residual",
    )(m, w, x, g_post.reshape(1, D), g_next.reshape(1, D))


def _mm_kernel(a_ref, b_ref, *rest, act, has_bias, head_major, scaled_tiles, scale):
    if has_bias:
        bias_ref, o_ref = rest
    else:
        (o_ref,) = rest
    acc = jnp.dot(a_ref[...], b_ref[...], preferred_element_type=F32)
    if has_bias:
        acc = acc + bias_ref[...]
    if scaled_tiles:
        acc = acc * jnp.where(pl.program_id(1) < scaled_tiles, scale, 1.0)
    if act == "sigmoid":
        acc = _sigmoid(acc)
    if head_major:
        for c in range(o_ref.shape[0]):
            o_ref[c] = acc[:, c * LANES:(c + 1) * LANES].astype(o_ref.dtype)
    else:
        o_ref[...] = acc.astype(o_ref.dtype)


def _mm(a, b, *, out_dtype, bm, bn, bias=None, act=None, head_major=False, scaled_cols=0, scale=1.0,
        cols=None):
    M, K = a.shape
    col0, N = cols if cols is not None else (0, b.shape[1])
    bm = _tile(M, bm)
    bn = _tile(N, bn)
    assert scaled_cols % bn == 0 and col0 % bn == 0
    j0 = col0 // bn
    in_specs = [pl.BlockSpec((bm, K), lambda i, j: (i, 0)), pl.BlockSpec((K, bn), lambda i, j: (0, j + j0))]
    args = [a, b]
    if bias is not None:
        in_specs.append(pl.BlockSpec((1, bn), lambda i, j: (0, j)))
        args.append(bias.reshape(1, N).astype(F32))
    if head_major:
        out_specs = pl.BlockSpec((bn // LANES, bm, LANES), lambda i, j: (j, i, 0))
        out_shape = jax.ShapeDtypeStruct((N // LANES, M, LANES), out_dtype)
    else:
        out_specs = pl.BlockSpec((bm, bn), lambda i, j: (i, j))
        out_shape = jax.ShapeDtypeStruct((M, N), out_dtype)
    return pl.pallas_call(
        functools.partial(_mm_kernel, act=act, has_bias=bias is not None, head_major=head_major,
                          scaled_tiles=scaled_cols // bn, scale=scale),
        grid=(M // bm, N // bn), in_specs=in_specs, out_specs=out_specs, out_shape=out_shape,
        compiler_params=_params("parallel", "arbitrary"), name="matmul",
    )(*args)


def _gated_out_kernel(na_ref, lru_ref, wna_ref, wlru_ref, gna_ref, glru_ref, o_ref):
    o_na = jnp.dot(na_ref[...], wna_ref[...], preferred_element_type=F32)
    o_lru = jnp.dot(lru_ref[...], wlru_ref[...], preferred_element_type=F32)
    o_ref[...] = (gna_ref[...].astype(F32) * o_na + glru_ref[...].astype(F32) * o_lru).astype(o_ref.dtype)


def _gated_out(na, lru, w_na, w_lru, gates, *, bm, bn):
    M, K = na.shape
    N = w_na.shape[1]
    bm = _tile(M, bm)
    bn = _tile(N, bn)
    nj = N // bn
    act = pl.BlockSpec((bm, K), lambda i, j: (i, 0))
    wgt = pl.BlockSpec((K, bn), lambda i, j: (0, j))
    return pl.pallas_call(
        _gated_out_kernel, grid=(M // bm, nj),
        in_specs=[act, act, wgt, wgt,
                  pl.BlockSpec((bm, bn), lambda i, j: (i, j)),
                  pl.BlockSpec((bm, bn), lambda i, j: (i, j + nj))],
        out_specs=pl.BlockSpec((bm, bn), lambda i, j: (i, j)),
        out_shape=jax.ShapeDtypeStruct((M, N), BF16),
        compiler_params=_params("parallel", "arbitrary"), name="gated_out",
    )(na, lru, w_na, w_lru, gates, gates)


def _swiglu_up_kernel(a_ref, wg_ref, wu_ref, o_ref):
    a = a_ref[...]
    g = jnp.dot(a, wg_ref[...], preferred_element_type=F32)
    u = jnp.dot(a, wu_ref[...], preferred_element_type=F32)
    o_ref[...] = (g * _sigmoid(g) * u).astype(o_ref.dtype)


def _swiglu_up(a, w_gate, w_up, *, bm, bn):
    M, K = a.shape
    N = w_gate.shape[1]
    bm = _tile(M, bm)
    bn = min(bn, N)
    wgt = pl.BlockSpec((K, bn), lambda i, j: (0, j))
    return pl.pallas_call(
        _swiglu_up_kernel, grid=(M // bm, pl.cdiv(N, bn)),
        in_specs=[pl.BlockSpec((bm, K), lambda i, j: (i, 0)), wgt, wgt],
        out_specs=pl.BlockSpec((bm, bn), lambda i, j: (i, j)),
        out_shape=jax.ShapeDtypeStruct((M, N), BF16),
        compiler_params=_params("parallel", "arbitrary"), name="swiglu_up",
    )(a, w_gate, w_up)


def _na_bias_tables(rpb, rows):
    heads = rpb.shape[0]
    rb = NA_ROWS_PER_BLOCK
    nb = rows // rb

    def valid(i):
        qr = i * rb + np.arange(rb)[:, None, None, None]
        qc = np.arange(GRID_W)[None, :, None, None]
        kr = (i - 1) * rb + np.arange(3 * rb)[None, None, :, None]
        kc = np.arange(GRID_W)[None, None, None, :]
        rs = np.clip(qr - WIN_ROWS // 2, 0, rows - WIN_ROWS)
        cs = np.clip(qc - WIN_COLS // 2, 0, GRID_W - WIN_COLS)
        return (kr >= rs) & (kr < rs + WIN_ROWS) & (kc >= cs) & (kc < cs + WIN_COLS)

    masks = np.stack([valid(i) for i in (0, 1, nb - 1)])
    dc = np.arange(GRID_W)[None, :] - np.arange(GRID_W)[:, None] + (WIN_COLS - 1)
    t = sum(jnp.where(dc == d, rpb[:, :, d][:, :, None, None].astype(F32), 0.0)
            for d in range(2 * WIN_COLS - 1))
    off = WIN_ROWS - 1 - rb
    tb = jnp.stack([jnp.swapaxes(t[:, off - a:off - a + 3 * rb], 1, 2) for a in range(rb)], axis=1)
    return jnp.where(masks[:, None], tb[None], MASK_VALUE).reshape(3, heads, NA_QBLK, NA_KBLK)


def _na_kernel(q_ref, kp_ref, kc_ref, kn_ref, vp_ref, vc_ref, vn_ref, b_ref, o_ref, kcat_ref, vext_ref):
    @pl.when(pl.program_id(0) == 0)
    def _():
        vext_ref[:, :, HEAD_DIM:] = jnp.ones(vext_ref.shape[:2] + (HEAD_DIM,), BF16)

    for h in range(q_ref.shape[0]):
        for seg, (k_ref, v_ref) in enumerate(((kp_ref, vp_ref), (kc_ref, vc_ref), (kn_ref, vn_ref))):
            rows = slice(seg * NA_QBLK, (seg + 1) * NA_QBLK)
            kcat_ref[h, rows, :] = k_ref[h]
            vext_ref[h, rows, :HEAD_DIM] = v_ref[h]
        s = lax.dot_general(q_ref[h], kcat_ref[h], (((1,), (1,)), ((), ())),
                            preferred_element_type=F32) + b_ref[0, h]
        m = jnp.max(s, axis=-1, keepdims=True)
        p = jnp.exp(s - m).astype(BF16)
        oe = jnp.dot(p, vext_ref[h], preferred_element_type=F32)
        o_ref[:, h * HEAD_DIM:(h + 1) * HEAD_DIM] = (oe[:, :HEAD_DIM] / oe[:, HEAD_DIM:]).astype(o_ref.dtype)


def _neighbourhood_attention(qkv, bias, heads, seq_lens):
    L = qkv.shape[1]
    nb = L // NA_QBLK
    blk = (heads, NA_QBLK, HEAD_DIM)
    bounds = _segment_bounds(seq_lens, NA_QBLK)
    assert bounds[-1] == nb

    def first_block(i):
        return sum(jnp.where(i >= hi, hi - lo, 0) for lo, hi in zip(bounds[:-2], bounds[1:-1]))

    def last_block(i):
        first = first_block(i)
        return sum(jnp.where(first == lo, hi - 1, 0) for lo, hi in zip(bounds[:-1], bounds[1:]))

    def prev_i(i):
        return jnp.maximum(i - 1, first_block(i))

    def next_i(i):
        return jnp.minimum(i + 1, last_block(i))

    def btype(i):
        return jnp.where(i == first_block(i), 0, jnp.where(i == last_block(i), 2, 1))

    in_specs = [
        pl.BlockSpec(blk, lambda i: (0, i, 0)),
        pl.BlockSpec(blk, lambda i: (1, prev_i(i), 0)),
        pl.BlockSpec(blk, lambda i: (1, i, 0)),
        pl.BlockSpec(blk, lambda i: (1, next_i(i), 0)),
        pl.BlockSpec(blk, lambda i: (2, prev_i(i), 0)),
        pl.BlockSpec(blk, lambda i: (2, i, 0)),
        pl.BlockSpec(blk, lambda i: (2, next_i(i), 0)),
        pl.BlockSpec((1, heads, NA_QBLK, NA_KBLK), lambda i: (btype(i), 0, 0, 0)),
    ]
    return pl.pallas_call(
        _na_kernel, grid=(nb,), in_specs=in_specs,
        out_specs=pl.BlockSpec((NA_QBLK, heads * HEAD_DIM), lambda i: (i, 0)),
        out_shape=jax.ShapeDtypeStruct((L, heads * HEAD_DIM), BF16),
        scratch_shapes=[pltpu.VMEM((heads, NA_KBLK, HEAD_DIM), BF16),
                        pltpu.VMEM((heads, NA_KBLK, 2 * HEAD_DIM), BF16)],
        compiler_params=_params("arbitrary"), name="neighbourhood_attention",
    )(qkv, qkv, qkv, qkv, qkv, qkv, qkv, bias)


def _softplus(z):
    return jnp.maximum(z, 0.0) + jnp.log1p(jnp.exp(-jnp.abs(z)))


def _phase_scan(a_phases, b_phases, order):
    prods, hs = [None] * SUBLANES, [None] * SUBLANES
    p = h = None
    for n, j in enumerate(order):
        if n == 0:
            p, h = a_phases[j], b_phases[j]
        else:
            h = a_phases[j] * h + b_phases[j]
            p = a_phases[j] * p
        prods[j], hs[j] = p, h
    return prods, hs


def _group_carries(pt, ht, carry, scratch, reverse):
    m = pt.shape[0]
    order = list(range(SUBLANES))[::-1] if reverse else list(range(SUBLANES))
    if m == SUBLANES:
        sub = lax.broadcasted_iota(jnp.int32, pt.shape, 0)
        a, b = pt, ht
        for k in (1, 2, 4):
            shift = SUBLANES - k if reverse else k
            a_sh = pltpu.roll(a, shift, 0)
            b_sh = pltpu.roll(b, shift, 0)
            valid = (sub + k < SUBLANES) if reverse else (sub >= k)
            b = jnp.where(valid, a * b_sh + b, b)
            a = jnp.where(valid, a * a_sh, a)
        after = a * carry + b
        edge = SUBLANES - 1 if reverse else 0
        before = jnp.where(sub == edge, carry, pltpu.roll(after, SUBLANES - 1 if reverse else 1, 0))
        last = 0 if reverse else SUBLANES - 1
        return before, jnp.broadcast_to(after[last:last + 1, :], carry.shape)

    pt_ref, ht_ref, cin_ref = scratch[0]
    pt_ref[...] = pt
    ht_ref[...] = ht
    sub_m = m // SUBLANES
    a_ph = [pt_ref[pl.ds(j, sub_m, stride=SUBLANES), :] for j in range(SUBLANES)]
    b_ph = [ht_ref[pl.ds(j, sub_m, stride=SUBLANES), :] for j in range(SUBLANES)]
    prods, hs = _phase_scan(a_ph, b_ph, order)
    inner, carry_out = _group_carries(prods[order[-1]], hs[order[-1]], carry, scratch[1:], reverse)
    for n, j in enumerate(order):
        if n == 0:
            c = inner
        else:
            prev = order[n - 1]
            c = prods[prev] * inner + hs[prev]
        cin_ref[pl.ds(j, sub_m, stride=SUBLANES), :] = c
    return cin_ref[...], carry_out


def _lru_kernel(*refs, reverse, starts, ends):
    if reverse:
        (xc_ref, wg_ref, br_ref, bi_ref, lam_ref, hf_ref, gr_ref, o_ref,
         carry_ref, xs_ref, p_ref, h_ref, *levels) = refs
    else:
        (x_ref, xp_ref, xn_ref, wc_ref, bc_ref, wg_ref, br_ref, bi_ref, lam_ref, o_ref, xc_ref,
         carry_ref, xs_ref, p_ref, h_ref, *levels) = refs
    scratch = [tuple(levels[i:i + 3]) for i in range(0, len(levels), 3)]
    tt = xc_ref.shape[0]
    groups = tt // SUBLANES
    step = pl.program_id(1)
    nt = pl.num_programs(1)
    ti = nt - 1 - step if reverse else step
    seq_begins = functools.reduce(jnp.logical_or, [ti == t for t in starts])
    seq_ends = functools.reduce(jnp.logical_or, [ti == t for t in ends])
    order = list(range(SUBLANES))[::-1] if reverse else list(range(SUBLANES))

    if reverse:
        xc = xc_ref[...]
    else:
        xs_ref[0:SUBLANES, :] = jnp.where(seq_begins, 0.0, xp_ref[...])
        xs_ref[SUBLANES:SUBLANES + tt, :] = x_ref[...]
        xs_ref[SUBLANES + tt:, :] = jnp.where(seq_ends, 0.0, xn_ref[...])

        xph = [xs_ref[pl.ds(SUBLANES + j, groups, stride=SUBLANES), :] for j in range(-2, SUBLANES + 1)]
        w = wc_ref[...]
        xc = jnp.concatenate(
            [xph[j] * w[0:1] + xph[j + 1] * w[1:2] + xph[j + 2] * w[2:3] + xph[j + 3] * w[3:4]
             for j in range(SUBLANES)], axis=0) + bc_ref[...]
        xc_ref[...] = xc

    g = jnp.dot(xc.astype(BF16), wg_ref[...], preferred_element_type=F32)
    r = _sigmoid(g[:, :LANES] + br_ref[...])
    ig = _sigmoid(g[:, LANES:] + bi_ref[...])

    log_a = r * (-LRU_C * _softplus(-lam_ref[...]))
    a = jnp.exp(log_a)
    th = jnp.tanh(log_a)
    n = -2.0 * th
    mult = n * lax.rsqrt(jnp.maximum(n * (1.0 - th), F32_TINY))
    u = ig * xc
    b = mult * u

    sub = lax.broadcasted_iota(jnp.int32, (SUBLANES, LANES), 0)
    if reverse:
        lo = tt - SUBLANES
        fix = (sub == SUBLANES - 1) & seq_ends
        b = jnp.concatenate([b[:lo], jnp.where(fix, u[lo:], b[lo:])], axis=0)
    else:
        fix = (sub == 0) & seq_begins
        b = jnp.concatenate([jnp.where(fix, u[:SUBLANES], b[:SUBLANES]), b[SUBLANES:]], axis=0)

    @pl.when(seq_ends if reverse else seq_begins)
    def _():
        carry_ref[...] = jnp.zeros_like(carry_ref)

    a_ph = [a[j * groups:(j + 1) * groups] for j in range(SUBLANES)]
    b_ph = [b[j * groups:(j + 1) * groups] for j in range(SUBLANES)]
    prods, hs = _phase_scan(a_ph, b_ph, order)
    for j in range(SUBLANES):
        p_ref[j * groups:(j + 1) * groups, :] = prods[j]
        h_ref[j * groups:(j + 1) * groups, :] = hs[j]
    cin, carry_out = _group_carries(prods[order[-1]], hs[order[-1]], carry_ref[...], scratch, reverse)
    carry_ref[...] = carry_out

    dst = xs_ref if reverse else o_ref
    for j in range(SUBLANES):
        rows = slice(j * groups, (j + 1) * groups)
        dst[pl.ds(j, groups, stride=SUBLANES), :] = p_ref[rows, :] * cin + h_ref[rows, :]
    if reverse:
        h = hf_ref[...] + xs_ref[0:tt, :]
        o_ref[...] = (h * jax.nn.gelu(gr_ref[...])).astype(o_ref.dtype)


def _rg_lru(xg, w_conv, b_conv, w_gates, b_r, b_i, lam, width, seq_lens):
    L = xg.shape[0]
    tt = SUBLANES
    while tt * SUBLANES <= 4096 and all(n % (tt * SUBLANES) == 0 for n in seq_lens):
        tt *= SUBLANES
    bounds = _segment_bounds(seq_lens, tt)
    starts, ends = tuple(bounds[:-1]), tuple(b - 1 for b in bounds[1:])
    assert tt >= SUBLANES * SUBLANES and width % LANES == 0
    nt = L // tt
    nc = width // LANES
    tpb = tt // SUBLANES
    nhalo = L // SUBLANES

    scratch_shapes = [pltpu.VMEM((SUBLANES, LANES), F32), pltpu.VMEM((tt + 2 * SUBLANES, LANES), F32),
                      pltpu.VMEM((tt, LANES), F32), pltpu.VMEM((tt, LANES), F32)]
    m = tt // SUBLANES
    while m > SUBLANES:
        scratch_shapes += [pltpu.VMEM((m, LANES), F32)] * 3
        m //= SUBLANES

    def run(reverse, head_in, head_specs, tail_in, tail_specs, out_dtypes):
        d = 1 if reverse else 0
        tix = (lambda i: nt - 1 - i) if reverse else (lambda i: i)
        tile = pl.BlockSpec((tt, LANES), lambda c, i: (tix(i), c))
        vec = pl.BlockSpec((1, LANES), lambda c, i: (0, c))
        in_specs = head_specs(tix, tile, vec) + [
            pl.BlockSpec((None, None, LANES, 2 * LANES), lambda c, i: (d, c, 0, 0)), vec, vec, vec,
        ] + tail_specs(tix, tile)
        out = pl.pallas_call(
            functools.partial(_lru_kernel, reverse=reverse, starts=starts, ends=ends),
            grid=(nc, nt), in_specs=in_specs,
            out_specs=[tile] * len(out_dtypes),
            out_shape=[jax.ShapeDtypeStruct((L, width), dt) for dt in out_dtypes],
            scratch_shapes=scratch_shapes,
            compiler_params=_params("parallel", "arbitrary"),
            name="rg_lru_rev" if reverse else "rg_lru_fwd",
        )(*head_in, w_gates, b_r[d].reshape(1, width), b_i[d].reshape(1, width), lam[d].reshape(1, width),
          *tail_in)
        return out

    h_fwd, xc = run(
        False, [xg, xg, xg, w_conv, b_conv.reshape(1, width)],
        lambda tix, tile, vec: [
            tile,
            pl.BlockSpec((SUBLANES, LANES), lambda c, i: (jnp.maximum(tix(i) * tpb - 1, 0), c)),
            pl.BlockSpec((SUBLANES, LANES), lambda c, i: (jnp.minimum((tix(i) + 1) * tpb, nhalo - 1), c)),
            pl.BlockSpec((CONV_W, LANES), lambda c, i: (0, c)),
            vec],
        [], lambda tix, tile: [], [F32, F32])
    (lru,) = run(
        True, [xc], lambda tix, tile, vec: [tile],
        [h_fwd, xg], lambda tix, tile: [tile, pl.BlockSpec((tt, LANES), lambda c, i: (tix(i), c + nc))],
        [BF16])
    return lru


def _prepare_weights(w_in, w_rgate, w_igate, w_na_out, w_lru_out, w_merge, w_out,
                     w_ffn_gate, w_ffn_up, w_ffn_down):
    w_gates = jnp.concatenate([w_rgate, w_igate], axis=-1).astype(BF16)
    return dict(
        w_in=w_in.astype(BF16),
        w_merge=w_merge.astype(BF16),
        w_gates=w_gates,
        w_na_out=w_na_out.astype(BF16),
        w_lru_out=w_lru_out.astype(BF16),
        w_out=w_out.astype(BF16),
        w_ffn_gate=w_ffn_gate.astype(BF16),
        w_ffn_up=w_ffn_up.astype(BF16),
        w_ffn_down=w_ffn_down.astype(BF16),
    )


def _encoder_layer(xs, w, w_conv, b_conv, b_rgate, b_igate, lru_lambda, rpb, b_merge,
                   g_mix_pre, g_mix_post, g_ffn_pre, g_ffn_post):
    seq_lens = [x.shape[0] for x in xs]
    heads = w["w_na_out"].shape[0] // HEAD_DIM
    lru_width = w["w_lru_out"].shape[0]
    min_rows = max(WIN_ROWS, 3 * NA_ROWS_PER_BLOCK)
    assert all(n % NA_QBLK == 0 and n // GRID_W >= min_rows for n in seq_lens)

    xn = _rmsnorm(xs, g_mix_pre, BF16)
    na_width = heads * HEAD_DIM
    bn_in = _tile(math.gcd(na_width, lru_width), 1024)
    qkv = _mm(xn, w["w_in"], cols=(0, 3 * na_width), out_dtype=BF16, bm=1024, bn=bn_in, head_major=True,
              scaled_cols=na_width, scale=HEAD_DIM ** -0.5)
    xg = _mm(xn, w["w_in"], cols=(3 * na_width, 2 * lru_width), out_dtype=F32, bm=1024, bn=bn_in)
    gates = _mm(xn, w["w_merge"], out_dtype=BF16, bm=1024, bn=1024, bias=b_merge, act="sigmoid")

    na = _neighbourhood_attention(qkv, _na_bias_tables(rpb, seq_lens[0] // GRID_W), heads, seq_lens)
    lru = _rg_lru(xg, w_conv, b_conv, w["w_gates"], b_rgate, b_igate, lru_lambda, lru_width, seq_lens)
    m = _gated_out(na, lru, w["w_na_out"], w["w_lru_out"], gates, bm=1024, bn=1024)

    ys, row0 = [], 0
    for x in xs:
        x1, x1n = _out_residual(m, w["w_out"], x, g_mix_post, g_ffn_pre, bm=256, parts=2, m_row0=row0)
        h = _swiglu_up(x1n, w["w_ffn_gate"], w["w_ffn_up"], bm=2048, bn=256)
        f = _mm(h, w["w_ffn_down"], out_dtype=BF16, bm=512, bn=512)
        ys.append(_residual(x1, f, g_ffn_post))
        row0 += x.shape[0]
    return ys


def kernel(x_prompt, x_sample, w_in, w_conv, b_conv, w_rgate, b_rgate, w_igate, b_igate, lru_lambda,
           rpb, w_na_out, w_lru_out, w_merge, b_merge, w_out, g_mix_pre, g_mix_post, g_ffn_pre,
           g_ffn_post, w_ffn_gate, w_ffn_up, w_ffn_down):
    n_prompt = x_prompt.shape[0]
    ys = [x_prompt[b] for b in range(n_prompt)] + [x_sample[b] for b in range(x_sample.shape[0])]
    for l in range(w_in.shape[0]):
        w = _prepare_weights(w_in[l], w_rgate[l], w_igate[l], w_na_out[l], w_lru_out[l], w_merge[l],
                             w_out[l], w_ffn_gate[l], w_ffn_up[l], w_ffn_down[l])
        ys = _encoder_layer(ys, w, w_conv[l], b_conv[l], b_rgate[l], b_igate[l], lru_lambda[l], rpb[l],
                            b_merge[l], g_mix_pre[l], g_mix_post[l], g_ffn_pre[l], g_ffn_post[l])

    def batch(seqs):
        return seqs[0][None] if len(seqs) == 1 else jnp.stack(seqs)

    return (batch(ys[:n_prompt]), batch(ys[n_prompt:]))
```

```python
import functools
import math

import jax
import jax.numpy as jnp
import numpy as np
from jax import lax
from jax.experimental import pallas as pl
from jax.experimental.pallas import tpu as pltpu

F32 = jnp.float32
BF16 = jnp.bfloat16

GRID_W = 64
HEAD_DIM = 128
WIN_ROWS = 8
WIN_COLS = 16
CONV_W = 4
LRU_C = 8.0
EPS = 1e-6

LANES = 128
SUBLANES = 8
VMEM_BYTES = 64 * 1024 * 1024
VMEM_LIMIT_BYTES = 56 * 1024 * 1024

NA_ROWS_PER_BLOCK = 4
NA_QBLK = NA_ROWS_PER_BLOCK * GRID_W
NA_KBLK = 3 * NA_QBLK
MASK_VALUE = -1e30
F32_TINY = float(np.finfo(np.float32).tiny)


def _params(*semantics, vmem_limit_bytes=VMEM_LIMIT_BYTES):
    return pltpu.CompilerParams(dimension_semantics=semantics, vmem_limit_bytes=vmem_limit_bytes)


def _tile(n, pref):
    if n <= pref:
        return n
    t = (pref // LANES) * LANES
    while t >= LANES:
        if n % t == 0:
            return t
        t -= LANES
    return n


def _rms(x):
    return x * lax.rsqrt(jnp.mean(x * x, axis=-1, keepdims=True) + EPS)


def _sigmoid(z):
    return 0.5 * jnp.tanh(0.5 * z) + 0.5


def _segment_bounds(lengths, unit):
    assert all(n % unit == 0 for n in lengths)
    bounds = [0]
    for n in lengths:
        bounds.append(bounds[-1] + n // unit)
    return bounds


def _rmsnorm_kernel(*refs, bounds):
    *x_refs, g_ref, o_ref = refs
    i = pl.program_id(0)
    for k, x_ref in enumerate(x_refs):
        @pl.when((i >= bounds[k]) & (i < bounds[k + 1]))
        def _(x_ref=x_ref):
            o_ref[...] = (_rms(x_ref[...]) * g_ref[...]).astype(o_ref.dtype)


def _rmsnorm(xs, g, out_dtype):
    D = xs[0].shape[1]
    bm = functools.reduce(math.gcd, [x.shape[0] for x in xs] + [512])
    bounds = _segment_bounds([x.shape[0] for x in xs], bm)

    def source(k):
        return pl.BlockSpec((bm, D), lambda i: (jnp.clip(i - bounds[k], 0, bounds[k + 1] - bounds[k] - 1), 0))

    return pl.pallas_call(
        functools.partial(_rmsnorm_kernel, bounds=bounds),
        grid=(bounds[-1],),
        in_specs=[source(k) for k in range(len(xs))] + [pl.BlockSpec((1, D), lambda i: (0, 0))],
        out_specs=pl.BlockSpec((bm, D), lambda i: (i, 0)),
        out_shape=jax.ShapeDtypeStruct((bounds[-1] * bm, D), out_dtype),
        compiler_params=_params("arbitrary"),
        name="rmsnorm",
    )(*xs, g.reshape(1, D))


def _residual_kernel(x_ref, y_ref, g_post_ref, o_ref):
    o_ref[...] = x_ref[...] + _rms(y_ref[...].astype(F32)) * g_post_ref[...]


def _residual(x, y, g_post):
    L, D = x.shape
    bm = _tile(L, 512)
    row = pl.BlockSpec((bm, D), lambda i: (i, 0))
    return pl.pallas_call(
        _residual_kernel, grid=(L // bm,), in_specs=[row, row, pl.BlockSpec((1, D), lambda i: (0, 0))],
        out_specs=row, out_shape=jax.ShapeDtypeStruct((L, D), F32), compiler_params=_params("parallel"),
        name="residual",
    )(x, y, g_post.reshape(1, D))


def _out_residual_kernel(m_ref, w_ref, x_ref, g_post_ref, g_next_ref, o_ref, on_ref, *, parts):
    rows = m_ref.shape[0] // parts
    for p in range(parts):
        sl = slice(p * rows, (p + 1) * rows)
        mix = jnp.dot(m_ref[sl, :], w_ref[...], preferred_element_type=F32)
        o = x_ref[sl, :] + _rms(mix) * g_post_ref[...]
        o_ref[sl, :] = o
        on_ref[sl, :] = (_rms(o) * g_next_ref[...]).astype(on_ref.dtype)


def _out_residual(m, w, x, g_post, g_next, *, bm, parts, m_row0=0):
    L, D = x.shape
    K = m.shape[1]
    bm = math.gcd(_tile(L, bm), m_row0) if m_row0 else _tile(L, bm)
    i0 = m_row0 // bm
    row = lambda width: pl.BlockSpec((bm, width), lambda i: (i, 0))
    vec = pl.BlockSpec((1, D), lambda i: (0, 0))
    need = 2 * K * D + 2 * bm * (2 * K + 10 * D) + 4 * (bm // parts) * D
    limit = min(VMEM_BYTES - 4 * 1024 * 1024, max(VMEM_LIMIT_BYTES, need + 2 * 1024 * 1024))
    return pl.pallas_call(
        functools.partial(_out_residual_kernel, parts=parts), grid=(L // bm,),
        in_specs=[pl.BlockSpec((bm, K), lambda i: (i + i0, 0)),
                  pl.BlockSpec((K, D), lambda i: (0, 0), pipeline_mode=pl.Buffered(1)),
                  row(D), vec, vec],
        out_specs=[row(D), row(D)],
        out_shape=[jax.ShapeDtypeStruct((L, D), F32), jax.ShapeDtypeStruct((L, D), BF16)],
        compiler_params=_params("parallel", vmem_limit_bytes=limit), name="out_residual",
    )(m, w, x, g_post.reshape(1, D), g_next.reshape(1, D))


def _mm_kernel(a_ref, b_ref, *rest, act, has_bias, head_major, scaled_tiles, scale):
    if has_bias:
        bias_ref, o_ref = rest
    else:
        (o_ref,) = rest
    acc = jnp.dot(a_ref[...], b_ref[...], preferred_element_type=F32)
    if has_bias:
        acc = acc + bias_ref[...]
    if scaled_tiles:
        acc = acc * jnp.where(pl.program_id(1) < scaled_tiles, scale, 1.0)
    if act == "sigmoid":
        acc = _sigmoid(acc)
    if head_major:
        for c in range(o_ref.shape[0]):
            o_ref[c] = acc[:, c * LANES:(c + 1) * LANES].astype(o_ref.dtype)
    else:
        o_ref[...] = acc.astype(o_ref.dtype)


def _mm(a, b, *, out_dtype, bm, bn, bias=None, act=None, head_major=False, scaled_cols=0, scale=1.0,
        cols=None):
    M, K = a.shape
    col0, N = cols if cols is not None else (0, b.shape[1])
    bm = _tile(M, bm)
    bn = _tile(N, bn)
    assert scaled_cols % bn == 0 and col0 % bn == 0
    j0 = col0 // bn
    in_specs = [pl.BlockSpec((bm, K), lambda i, j: (i, 0)), pl.BlockSpec((K, bn), lambda i, j: (0, j + j0))]
    args = [a, b]
    if bias is not None:
        in_specs.append(pl.BlockSpec((1, bn), lambda i, j: (0, j)))
        args.append(bias.reshape(1, N).astype(F32))
    if head_major:
        out_specs = pl.BlockSpec((bn // LANES, bm, LANES), lambda i, j: (j, i, 0))
        out_shape = jax.ShapeDtypeStruct((N // LANES, M, LANES), out_dtype)
    else:
        out_specs = pl.BlockSpec((bm, bn), lambda i, j: (i, j))
        out_shape = jax.ShapeDtypeStruct((M, N), out_dtype)
    return pl.pallas_call(
        functools.partial(_mm_kernel, act=act, has_bias=bias is not None, head_major=head_major,
                          scaled_tiles=scaled_cols // bn, scale=scale),
        grid=(M // bm, N // bn), in_specs=in_specs, out_specs=out_specs, out_shape=out_shape,
        compiler_params=_params("parallel", "arbitrary"), name="matmul",
    )(*args)


def _gated_out_kernel(na_ref, lru_ref, wna_ref, wlru_ref, gna_ref, glru_ref, o_ref):
    o_na = jnp.dot(na_ref[...], wna_ref[...], preferred_element_type=F32)
    o_lru = jnp.dot(lru_ref[...], wlru_ref[...], preferred_element_type=F32)
    o_ref[...] = (gna_ref[...].astype(F32) * o_na + glru_ref[...].astype(F32) * o_lru).astype(o_ref.dtype)


def _gated_out(na, lru, w_na, w_lru, gates, *, bm, bn):
    M, K = na.shape
    N = w_na.shape[1]
    bm = _tile(M, bm)
    bn = _tile(N, bn)
    nj = N // bn
    act = pl.BlockSpec((bm, K), lambda i, j: (i, 0))
    wgt = pl.BlockSpec((K, bn), lambda i, j: (0, j))
    return pl.pallas_call(
        _gated_out_kernel, grid=(M // bm, nj),
        in_specs=[act, act, wgt, wgt,
                  pl.BlockSpec((bm, bn), lambda i, j: (i, j)),
                  pl.BlockSpec((bm, bn), lambda i, j: (i, j + nj))],
        out_specs=pl.BlockSpec((bm, bn), lambda i, j: (i, j)),
        out_shape=jax.ShapeDtypeStruct((M, N), BF16),
        compiler_params=_params("parallel", "arbitrary"), name="gated_out",
    )(na, lru, w_na, w_lru, gates, gates)


def _swiglu_up_kernel(a_ref, wg_ref, wu_ref, o_ref):
    a = a_ref[...]
    g = jnp.dot(a, wg_ref[...], preferred_element_type=F32)
    u = jnp.dot(a, wu_ref[...], preferred_element_type=F32)
    o_ref[...] = (g * _sigmoid(g) * u).astype(o_ref.dtype)


def _swiglu_up(a, w_gate, w_up, *, bm, bn):
    M, K = a.shape
    N = w_gate.shape[1]
    bm = _tile(M, bm)
    bn = min(bn, N)
    wgt = pl.BlockSpec((K, bn), lambda i, j: (0, j))
    return pl.pallas_call(
        _swiglu_up_kernel, grid=(M // bm, pl.cdiv(N, bn)),
        in_specs=[pl.BlockSpec((bm, K), lambda i, j: (i, 0)), wgt, wgt],
        out_specs=pl.BlockSpec((bm, bn), lambda i, j: (i, j)),
        out_shape=jax.ShapeDtypeStruct((M, N), BF16),
        compiler_params=_params("parallel", "arbitrary"), name="swiglu_up",
    )(a, w_gate, w_up)


def _na_bias_tables(rpb, rows):
    heads = rpb.shape[0]
    rb = NA_ROWS_PER_BLOCK
    nb = rows // rb

    def valid(i):
        qr = i * rb + np.arange(rb)[:, None, None, None]
        qc = np.arange(GRID_W)[None, :, None, None]
        kr = (i - 1) * rb + np.arange(3 * rb)[None, None, :, None]
        kc = np.arange(GRID_W)[None, None, None, :]
        rs = np.clip(qr - WIN_ROWS // 2, 0, rows - WIN_ROWS)
        cs = np.clip(qc - WIN_COLS // 2, 0, GRID_W - WIN_COLS)
        return (kr >= rs) & (kr < rs + WIN_ROWS) & (kc >= cs) & (kc < cs + WIN_COLS)

    masks = np.stack([valid(i) for i in (0, 1, nb - 1)])
    dc = np.arange(GRID_W)[None, :] - np.arange(GRID_W)[:, None] + (WIN_COLS - 1)
    t = sum(jnp.where(dc == d, rpb[:, :, d][:, :, None, None].astype(F32), 0.0)
            for d in range(2 * WIN_COLS - 1))
    off = WIN_ROWS - 1 - rb
    tb = jnp.stack([jnp.swapaxes(t[:, off - a:off - a + 3 * rb], 1, 2) for a in range(rb)], axis=1)
    return jnp.where(masks[:, None], tb[None], MASK_VALUE).reshape(3, heads, NA_QBLK, NA_KBLK)


def _na_kernel(q_ref, kp_ref, kc_ref, kn_ref, vp_ref, vc_ref, vn_ref, b_ref, o_ref, kcat_ref, vext_ref):
    @pl.when(pl.program_id(0) == 0)
    def _():
        vext_ref[:, :, HEAD_DIM:] = jnp.ones(vext_ref.shape[:2] + (HEAD_DIM,), BF16)

    for h in range(q_ref.shape[0]):
        for seg, (k_ref, v_ref) in enumerate(((kp_ref, vp_ref), (kc_ref, vc_ref), (kn_ref, vn_ref))):
            rows = slice(seg * NA_QBLK, (seg + 1) * NA_QBLK)
            kcat_ref[h, rows, :] = k_ref[h]
            vext_ref[h, rows, :HEAD_DIM] = v_ref[h]
        s = lax.dot_general(q_ref[h], kcat_ref[h], (((1,), (1,)), ((), ())),
                            preferred_element_type=F32) + b_ref[0, h]
        m = jnp.max(s, axis=-1, keepdims=True)
        p = jnp.exp(s - m).astype(BF16)
        oe = jnp.dot(p, vext_ref[h], preferred_element_type=F32)
        o_ref[:, h * HEAD_DIM:(h + 1) * HEAD_DIM] = (oe[:, :HEAD_DIM] / oe[:, HEAD_DIM:]).astype(o_ref.dtype)


def _neighbourhood_attention(qkv, bias, heads, seq_lens):
    L = qkv.shape[1]
    nb = L // NA_QBLK
    blk = (heads, NA_QBLK, HEAD_DIM)
    bounds = _segment_bounds(seq_lens, NA_QBLK)
    assert bounds[-1] == nb

    def first_block(i):
        return sum(jnp.where(i >= hi, hi - lo, 0) for lo, hi in zip(bounds[:-2], bounds[1:-1]))

    def last_block(i):
        first = first_block(i)
        return sum(jnp.where(first == lo, hi - 1, 0) for lo, hi in zip(bounds[:-1], bounds[1:]))

    def prev_i(i):
        return jnp.maximum(i - 1, first_block(i))

    def next_i(i):
        return jnp.minimum(i + 1, last_block(i))

    def btype(i):
        return jnp.where(i == first_block(i), 0, jnp.where(i == last_block(i), 2, 1))

    in_specs = [
        pl.BlockSpec(blk, lambda i: (0, i, 0)),
        pl.BlockSpec(blk, lambda i: (1, prev_i(i), 0)),
        pl.BlockSpec(blk, lambda i: (1, i, 0)),
        pl.BlockSpec(blk, lambda i: (1, next_i(i), 0)),
        pl.BlockSpec(blk, lambda i: (2, prev_i(i), 0)),
        pl.BlockSpec(blk, lambda i: (2, i, 0)),
        pl.BlockSpec(blk, lambda i: (2, next_i(i), 0)),
        pl.BlockSpec((1, heads, NA_QBLK, NA_KBLK), lambda i: (btype(i), 0, 0, 0)),
    ]
    return pl.pallas_call(
        _na_kernel, grid=(nb,), in_specs=in_specs,
        out_specs=pl.BlockSpec((NA_QBLK, heads * HEAD_DIM), lambda i: (i, 0)),
        out_shape=jax.ShapeDtypeStruct((L, heads * HEAD_DIM), BF16),
        scratch_shapes=[pltpu.VMEM((heads, NA_KBLK, HEAD_DIM), BF16),
                        pltpu.VMEM((heads, NA_KBLK, 2 * HEAD_DIM), BF16)],
        compiler_params=_params("arbitrary"), name="neighbourhood_attention",
    )(qkv, qkv, qkv, qkv, qkv, qkv, qkv, bias)


def _softplus(z):
    return jnp.maximum(z, 0.0) + jnp.log1p(jnp.exp(-jnp.abs(z)))


def _phase_scan(a_phases, b_phases, order):
    prods, hs = [None] * SUBLANES, [None] * SUBLANES
    p = h = None
    for n, j in enumerate(order):
        if n == 0:
            p, h = a_phases[j], b_phases[j]
        else:
            h = a_phases[j] * h + b_phases[j]
            p = a_phases[j] * p
        prods[j], hs[j] = p, h
    return prods, hs


def _group_carries(pt, ht, carry, scratch, reverse):
    m = pt.shape[0]
    order = list(range(SUBLANES))[::-1] if reverse else list(range(SUBLANES))
    if m == SUBLANES:
        sub = lax.broadcasted_iota(jnp.int32, pt.shape, 0)
        a, b = pt, ht
        for k in (1, 2, 4):
            shift = SUBLANES - k if reverse else k
            a_sh = pltpu.roll(a, shift, 0)
            b_sh = pltpu.roll(b, shift, 0)
            valid = (sub + k < SUBLANES) if reverse else (sub >= k)
            b = jnp.where(valid, a * b_sh + b, b)
            a = jnp.where(valid, a * a_sh, a)
        after = a * carry + b
        edge = SUBLANES - 1 if reverse else 0
        before = jnp.where(sub == edge, carry, pltpu.roll(after, SUBLANES - 1 if reverse else 1, 0))
        last = 0 if reverse else SUBLANES - 1
        return before, jnp.broadcast_to(after[last:last + 1, :], carry.shape)

    pt_ref, ht_ref, cin_ref = scratch[0]
    pt_ref[...] = pt
    ht_ref[...] = ht
    sub_m = m // SUBLANES
    a_ph = [pt_ref[pl.ds(j, sub_m, stride=SUBLANES), :] for j in range(SUBLANES)]
    b_ph = [ht_ref[pl.ds(j, sub_m, stride=SUBLANES), :] for j in range(SUBLANES)]
    prods, hs = _phase_scan(a_ph, b_ph, order)
    inner, carry_out = _group_carries(prods[order[-1]], hs[order[-1]], carry, scratch[1:], reverse)
    for n, j in enumerate(order):
        if n == 0:
            c = inner
        else:
            prev = order[n - 1]
            c = prods[prev] * inner + hs[prev]
        cin_ref[pl.ds(j, sub_m, stride=SUBLANES), :] = c
    return cin_ref[...], carry_out


def _lru_kernel(*refs, reverse, starts, ends):
    if reverse:
        (xc_ref, wg_ref, br_ref, bi_ref, lam_ref, hf_ref, gr_ref, o_ref,
         carry_ref, xs_ref, p_ref, h_ref, *levels) = refs
    else:
        (x_ref, xp_ref, xn_ref, wc_ref, bc_ref, wg_ref, br_ref, bi_ref, lam_ref, o_ref, xc_ref,
         carry_ref, xs_ref, p_ref, h_ref, *levels) = refs
    scratch = [tuple(levels[i:i + 3]) for i in range(0, len(levels), 3)]
    tt = xc_ref.shape[0]
    groups = tt // SUBLANES
    step = pl.program_id(1)
    nt = pl.num_programs(1)
    ti = nt - 1 - step if reverse else step
    seq_begins = functools.reduce(jnp.logical_or, [ti == t for t in starts])
    seq_ends = functools.reduce(jnp.logical_or, [ti == t for t in ends])
    order = list(range(SUBLANES))[::-1] if reverse else list(range(SUBLANES))

    if reverse:
        xc = xc_ref[...]
    else:
        xs_ref[0:SUBLANES, :] = jnp.where(seq_begins, 0.0, xp_ref[...])
        xs_ref[SUBLANES:SUBLANES + tt, :] = x_ref[...]
        xs_ref[SUBLANES + tt:, :] = jnp.where(seq_ends, 0.0, xn_ref[...])

        xph = [xs_ref[pl.ds(SUBLANES + j, groups, stride=SUBLANES), :] for j in range(-2, SUBLANES + 1)]
        w = wc_ref[...]
        xc = jnp.concatenate(
            [xph[j] * w[0:1] + xph[j + 1] * w[1:2] + xph[j + 2] * w[2:3] + xph[j + 3] * w[3:4]
             for j in range(SUBLANES)], axis=0) + bc_ref[...]
        xc_ref[...] = xc

    g = jnp.dot(xc.astype(BF16), wg_ref[...], preferred_element_type=F32)
    r = _sigmoid(g[:, :LANES] + br_ref[...])
    ig = _sigmoid(g[:, LANES:] + bi_ref[...])

    log_a = r * (-LRU_C * _softplus(-lam_ref[...]))
    a = jnp.exp(log_a)
    th = jnp.tanh(log_a)
    n = -2.0 * th
    mult = n * lax.rsqrt(jnp.maximum(n * (1.0 - th), F32_TINY))
    u = ig * xc
    b = mult * u

    sub = lax.broadcasted_iota(jnp.int32, (SUBLANES, LANES), 0)
    if reverse:
        lo = tt - SUBLANES
        fix = (sub == SUBLANES - 1) & seq_ends
        b = jnp.concatenate([b[:lo], jnp.where(fix, u[lo:], b[lo:])], axis=0)
    else:
        fix = (sub == 0) & seq_begins
        b = jnp.concatenate([jnp.where(fix, u[:SUBLANES], b[:SUBLANES]), b[SUBLANES:]], axis=0)

    @pl.when(seq_ends if reverse else seq_begins)
    def _():
        carry_ref[...] = jnp.zeros_like(carry_ref)

    a_ph = [a[j * groups:(j + 1) * groups] for j in range(SUBLANES)]
    b_ph = [b[j * groups:(j + 1) * groups] for j in range(SUBLANES)]
    prods, hs = _phase_scan(a_ph, b_ph, order)
    for j in range(SUBLANES):
        p_ref[j * groups:(j + 1) * groups, :] = prods[j]
        h_ref[j * groups:(j + 1) * groups, :] = hs[j]
    cin, carry_out = _group_carries(prods[order[-1]], hs[order[-1]], carry_ref[...], scratch, reverse)
    carry_ref[...] = carry_out

    dst = xs_ref if reverse else o_ref
    for j in range(SUBLANES):
        rows = slice(j * groups, (j + 1) * groups)
        dst[pl.ds(j, groups, stride=SUBLANES), :] = p_ref[rows, :] * cin + h_ref[rows, :]
    if reverse:
        h = hf_ref[...] + xs_ref[0:tt, :]
        o_ref[...] = (h * jax.nn.gelu(gr_ref[...])).astype(o_ref.dtype)


def _rg_lru(xg, w_conv, b_conv, w_gates, b_r, b_i, lam, width, seq_lens):
    L = xg.shape[0]
    tt = SUBLANES
    while tt * SUBLANES <= 4096 and all(n % (tt * SUBLANES) == 0 for n in seq_lens):
        tt *= SUBLANES
    bounds = _segment_bounds(seq_lens, tt)
    starts, ends = tuple(bounds[:-1]), tuple(b - 1 for b in bounds[1:])
    assert tt >= SUBLANES * SUBLANES and width % LANES == 0
    nt = L // tt
    nc = width // LANES
    tpb = tt // SUBLANES
    nhalo = L // SUBLANES

    scratch_shapes = [pltpu.VMEM((SUBLANES, LANES), F32), pltpu.VMEM((tt + 2 * SUBLANES, LANES), F32),
                      pltpu.VMEM((tt, LANES), F32), pltpu.VMEM((tt, LANES), F32)]
    m = tt // SUBLANES
    while m > SUBLANES:
        scratch_shapes += [pltpu.VMEM((m, LANES), F32)] * 3
        m //= SUBLANES

    def run(reverse, head_in, head_specs, tail_in, tail_specs, out_dtypes):
        d = 1 if reverse else 0
        tix = (lambda i: nt - 1 - i) if reverse else (lambda i: i)
        tile = pl.BlockSpec((tt, LANES), lambda c, i: (tix(i), c))
        vec = pl.BlockSpec((1, LANES), lambda c, i: (0, c))
        in_specs = head_specs(tix, tile, vec) + [
            pl.BlockSpec((None, None, LANES, 2 * LANES), lambda c, i: (d, c, 0, 0)), vec, vec, vec,
        ] + tail_specs(tix, tile)
        out = pl.pallas_call(
            functools.partial(_lru_kernel, reverse=reverse, starts=starts, ends=ends),
            grid=(nc, nt), in_specs=in_specs,
            out_specs=[tile] * len(out_dtypes),
            out_shape=[jax.ShapeDtypeStruct((L, width), dt) for dt in out_dtypes],
            scratch_shapes=scratch_shapes,
            compiler_params=_params("parallel", "arbitrary"),
            name="rg_lru_rev" if reverse else "rg_lru_fwd",
        )(*head_in, w_gates, b_r[d].reshape(1, width), b_i[d].reshape(1, width), lam[d].reshape(1, width),
          *tail_in)
        return out

    h_fwd, xc = run(
        False, [xg, xg, xg, w_conv, b_conv.reshape(1, width)],
        lambda tix, tile, vec: [
            tile,
            pl.BlockSpec((SUBLANES, LANES), lambda c, i: (jnp.maximum(tix(i) * tpb - 1, 0), c)),
            pl.BlockSpec((SUBLANES, LANES), lambda c, i: (jnp.minimum((tix(i) + 1) * tpb, nhalo - 1), c)),
            pl.BlockSpec((CONV_W, LANES), lambda c, i: (0, c)),
            vec],
        [], lambda tix, tile: [], [F32, F32])
    (lru,) = run(
        True, [xc], lambda tix, tile, vec: [tile],
        [h_fwd, xg], lambda tix, tile: [tile, pl.BlockSpec((tt, LANES), lambda c, i: (tix(i), c + nc))],
        [BF16])
    return lru


def _prepare_weights(w_in, w_rgate, w_igate, w_na_out, w_lru_out, w_merge, w_out,
                     w_ffn_gate, w_ffn_up, w_ffn_down):
    w_gates = jnp.concatenate([w_rgate, w_igate], axis=-1).astype(BF16)
    return dict(
        w_in=w_in.astype(BF16),
        w_merge=w_merge.astype(BF16),
        w_gates=w_gates,
        w_na_out=w_na_out.astype(BF16),
        w_lru_out=w_lru_out.astype(BF16),
        w_out=w_out.astype(BF16),
        w_ffn_gate=w_ffn_gate.astype(BF16),
        w_ffn_up=w_ffn_up.astype(BF16),
        w_ffn_down=w_ffn_down.astype(BF16),
    )


def _encoder_layer(xs, w, w_conv, b_conv, b_rgate, b_igate, lru_lambda, rpb, b_merge,
                   g_mix_pre, g_mix_post, g_ffn_pre, g_ffn_post):
    seq_lens = [x.shape[0] for x in xs]
    heads = w["w_na_out"].shape[0] // HEAD_DIM
    lru_width = w["w_lru_out"].shape[0]
    min_rows = max(WIN_ROWS, 3 * NA_ROWS_PER_BLOCK)
    assert all(n % NA_QBLK == 0 and n // GRID_W >= min_rows for n in seq_lens)

    xn = _rmsnorm(xs, g_mix_pre, BF16)
    na_width = heads * HEAD_DIM
    bn_in = _tile(math.gcd(na_width, lru_width), 1024)
    qkv = _mm(xn, w["w_in"], cols=(0, 3 * na_width), out_dtype=BF16, bm=1024, bn=bn_in, head_major=True,
              scaled_cols=na_width, scale=HEAD_DIM ** -0.5)
    xg = _mm(xn, w["w_in"], cols=(3 * na_width, 2 * lru_width), out_dtype=F32, bm=1024, bn=bn_in)
    gates = _mm(xn, w["w_merge"], out_dtype=BF16, bm=1024, bn=1024, bias=b_merge, act="sigmoid")

    na = _neighbourhood_attention(qkv, _na_bias_tables(rpb, seq_lens[0] // GRID_W), heads, seq_lens)
    lru = _rg_lru(xg, w_conv, b_conv, w["w_gates"], b_rgate, b_igate, lru_lambda, lru_width, seq_lens)
    m = _gated_out(na, lru, w["w_na_out"], w["w_lru_out"], gates, bm=1024, bn=1024)

    ys, row0 = [], 0
    for x in xs:
        x1, x1n = _out_residual(m, w["w_out"], x, g_mix_post, g_ffn_pre, bm=256, parts=2, m_row0=row0)
        h = _swiglu_up(x1n, w["w_ffn_gate"], w["w_ffn_up"], bm=2048, bn=256)
        f = _mm(h, w["w_ffn_down"], out_dtype=BF16, bm=512, bn=512)
        ys.append(_residual(x1, f, g_ffn_post))
        row0 += x.shape[0]
    return ys


def kernel(x_prompt, x_sample, w_in, w_conv, b_conv, w_rgate, b_rgate, w_igate, b_igate, lru_lambda,
           rpb, w_na_out, w_lru_out, w_merge, b_merge, w_out, g_mix_pre, g_mix_post, g_ffn_pre,
           g_ffn_post, w_ffn_gate, w_ffn_up, w_ffn_down):
    n_prompt = x_prompt.shape[0]
    ys = [x_prompt[b] for b in range(n_prompt)] + [x_sample[b] for b in range(x_sample.shape[0])]
    for l in range(w_in.shape[0]):
        w = _prepare_weights(w_in[l], w_rgate[l], w_igate[l], w_na_out[l], w_lru_out[l], w_merge[l],
                             w_out[l], w_ffn_gate[l], w_ffn_up[l], w_ffn_down[l])
        ys = _encoder_layer(ys, w, w_conv[l], b_conv[l], b_rgate[l], b_igate[l], lru_lambda[l], rpb[l],
                            b_merge[l], g_mix_pre[l], g_mix_post[l], g_ffn_pre[l], g_ffn_post[l])

    def batch(seqs):
        return seqs[0][None] if len(seqs) == 1 else jnp.stack(seqs)

    return (batch(ys[:n_prompt]), batch(ys[n_prompt:]))
```
